```python
import math
import jax
import jax.numpy as jnp
from jax import lax
import numpy as np

D_MODEL = 1024
BATCH = 2
SEQ = 8192
DEPTH = 2

N_HEADS = 8
HEAD_DIM = D_MODEL // (2 * N_HEADS)
V_DIM = 2 * HEAD_DIM
QK_WIDTH = N_HEADS * 2 * HEAD_DIM
ATTN_WIDTH = N_HEADS * V_DIM
RNN_WIDTH = D_MODEL
RNN_BLOCKS = 8
RNN_BLOCK = RNN_WIDTH // RNN_BLOCKS
RNN_CONV = 4
RGLRU_C = 8.0
D_FF = 3 * D_MODEL
FFN_CONV = 3
ROPE_THETA = 10000.0
Q_BLOCK = 128
EPS = 1e-6
N_MOD = 6
IN_WIDTH = 2 * QK_WIDTH + ATTN_WIDTH + 2 * RNN_WIDTH + 2 * D_MODEL

kernel_name = 'hybrid_rglru_diffattn_convffn_adaln'


def rms_norm(x, g):
    xf = x.astype(jnp.float32)
    y = xf * lax.rsqrt(jnp.mean(xf * xf, axis=-1, keepdims=True) + EPS)
    return (y * g.astype(jnp.float32)).astype(x.dtype)


def rope_tables(positions):
    inv_freq = ROPE_THETA ** (-jnp.arange(0, HEAD_DIM, 2, dtype=jnp.float32) / HEAD_DIM)
    ang = positions.astype(jnp.float32)[..., None] * inv_freq
    return jnp.cos(ang), jnp.sin(ang)


def apply_rope(t, cos, sin):
    tf = t.astype(jnp.float32)
    t1, t2 = jnp.split(tf, 2, axis=-1)
    cb = cos[:, :, None, None, :]
    sb = sin[:, :, None, None, :]
    return jnp.concatenate([t1 * cb - t2 * sb, t2 * cb + t1 * sb], axis=-1).astype(t.dtype)


def causal_dwconv(x, w, b):
    width = w.shape[0]
    s = x.shape[1]
    xp = jnp.pad(x, ((0, 0), (width - 1, 0), (0, 0)))
    y = b
    for j in range(width):
        y = y + xp[:, j:j + s, :] * w[j]
    return y


def rg_lru(x, w_a, b_a, w_i, b_i, lam):
    bsz, s, width = x.shape
    xf = x.astype(jnp.float32)
    xb = xf.reshape(bsz, s, RNN_BLOCKS, RNN_BLOCK)
    r = jax.nn.sigmoid(jnp.einsum('bsnk,nkj->bsnj', xb, w_a.astype(jnp.float32)).reshape(bsz, s, width)
                       + b_a.astype(jnp.float32))
    i = jax.nn.sigmoid(jnp.einsum('bsnk,nkj->bsnj', xb, w_i.astype(jnp.float32)).reshape(bsz, s, width)
                       + b_i.astype(jnp.float32))
    log_a = -RGLRU_C * r * jax.nn.softplus(-lam.astype(jnp.float32))
    a = jnp.exp(log_a)
    u = jnp.sqrt(-jnp.expm1(2.0 * log_a)) * (i * xf)

    def combine(lhs, rhs):
        a_l, h_l = lhs
        a_r, h_r = rhs
        return a_l * a_r, a_r * h_l + h_r

    _, h = lax.associative_scan(combine, (a, u), axis=1)
    return h.astype(x.dtype)


def diff_attention(q, k, v, lam, g_subln, out_scale):
    bsz, s, n_h, _, d_h = q.shape
    n_blk = s // Q_BLOCK
    qb = (q * (d_h ** -0.5)).reshape(bsz, n_blk, Q_BLOCK, n_h, 2, d_h)
    qb = jnp.moveaxis(qb, 1, 0)
    key_pos = jnp.arange(s)

    def one_block(args):
        q_blk, blk = args
        scores = jnp.einsum('bqhcd,bkhcd->bhcqk', q_blk, k, preferred_element_type=jnp.float32)
        query_pos = blk * Q_BLOCK + jnp.arange(Q_BLOCK)
        causal = key_pos[None, :] <= query_pos[:, None]
        scores = jnp.where(causal, scores, -jnp.inf)
        probs = jax.nn.softmax(scores, axis=-1)
        weights = probs[:, :, 0] - lam * probs[:, :, 1]
        return jnp.einsum('bhqk,bkhd->bqhd', weights.astype(v.dtype), v)

    o = lax.map(one_block, (qb, jnp.arange(n_blk)))
    o = jnp.moveaxis(o, 0, 1).reshape(bsz, s, n_h, V_DIM)
    o = rms_norm(o, g_subln) * out_scale
    return o.reshape(bsz, s, n_h * V_DIM)


def setup_inputs(seed: int = 0) -> dict:
    key = jax.random.key(seed)
    ks = jax.random.split(key, 26)

    def nrm(k, shape, scale):
        return jax.random.normal(k, shape, jnp.float32) * scale

    x = nrm(ks[0], (BATCH, SEQ, D_MODEL), 1.0)
    c = nrm(ks[1], (BATCH, D_MODEL), 1.0)
    positions = (jnp.arange(SEQ, dtype=jnp.int32)[None, :]
                 + jax.random.randint(ks[2], (BATCH, 1), 0, 1024, dtype=jnp.int32))
    w_mod = nrm(ks[3], (DEPTH, D_MODEL, N_MOD * D_MODEL), 0.5 * D_MODEL ** -0.5)
    b_mod = nrm(ks[4], (DEPTH, N_MOD * D_MODEL), 0.01)
    g_norm_mix = 1.0 + nrm(ks[5], (DEPTH, D_MODEL), 0.02)
    w_in = nrm(ks[6], (DEPTH, D_MODEL, IN_WIDTH), D_MODEL ** -0.5)
    lam_q1 = nrm(ks[7], (DEPTH, HEAD_DIM), 0.1)
    lam_k1 = nrm(ks[8], (DEPTH, HEAD_DIM), 0.1)
    lam_q2 = nrm(ks[9], (DEPTH, HEAD_DIM), 0.1)
    lam_k2 = nrm(ks[10], (DEPTH, HEAD_DIM), 0.1)
    g_subln = 1.0 + nrm(ks[11], (DEPTH, V_DIM), 0.02)
    conv_rnn_w = nrm(ks[12], (DEPTH, RNN_CONV, RNN_WIDTH), RNN_CONV ** -0.5)
    conv_rnn_b = nrm(ks[13], (DEPTH, RNN_WIDTH), 0.01)
    w_rg_a = nrm(ks[14], (DEPTH, RNN_BLOCKS, RNN_BLOCK, RNN_BLOCK), RNN_BLOCK ** -0.5)
    b_rg_a = nrm(ks[15], (DEPTH, RNN_WIDTH), 0.01)
    w_rg_i = nrm(ks[16], (DEPTH, RNN_BLOCKS, RNN_BLOCK, RNN_BLOCK), RNN_BLOCK ** -0.5)
    b_rg_i = nrm(ks[17], (DEPTH, RNN_WIDTH), 0.01)
    a_target = jax.random.uniform(ks[18], (DEPTH, RNN_WIDTH), jnp.float32, 0.9, 0.999)
    base = a_target ** (1.0 / RGLRU_C)
    lam_rg = jnp.log(base) - jnp.log1p(-base)
    w_out = nrm(ks[19], (DEPTH, D_MODEL, D_MODEL), D_MODEL ** -0.5)
    g_norm_ffn = 1.0 + nrm(ks[20], (DEPTH, D_MODEL), 0.02)
    w_up = nrm(ks[21], (DEPTH, D_MODEL, 2 * D_FF), D_MODEL ** -0.5)
    conv_ffn_w = nrm(ks[22], (DEPTH, FFN_CONV, 2 * D_FF), FFN_CONV ** -0.5)
    conv_ffn_b = nrm(ks[23], (DEPTH, 2 * D_FF), 0.01)
    w_down = nrm(ks[24], (DEPTH, D_FF, D_MODEL), D_FF ** -0.5)
    g_final = 1.0 + nrm(ks[25], (D_MODEL,), 0.02)
    return {'x': x, 'c': c, 'positions': positions, 'w_mod': w_mod, 'b_mod': b_mod,
            'g_norm_mix': g_norm_mix, 'w_in': w_in, 'lam_q1': lam_q1, 'lam_k1': lam_k1,
            'lam_q2': lam_q2, 'lam_k2': lam_k2, 'g_subln': g_subln,
            'conv_rnn_w': conv_rnn_w, 'conv_rnn_b': conv_rnn_b, 'w_rg_a': w_rg_a,
            'b_rg_a': b_rg_a, 'w_rg_i': w_rg_i, 'b_rg_i': b_rg_i, 'lam_rg': lam_rg,
            'w_out': w_out, 'g_norm_ffn': g_norm_ffn, 'w_up': w_up,
            'conv_ffn_w': conv_ffn_w, 'conv_ffn_b': conv_ffn_b, 'w_down': w_down,
            'g_final': g_final}


def reference(x, c, positions, w_mod, b_mod, g_norm_mix, w_in, lam_q1, lam_k1, lam_q2, lam_k2,
              g_subln, conv_rnn_w, conv_rnn_b, w_rg_a, b_rg_a, w_rg_i, b_rg_i, lam_rg,
              w_out, g_norm_ffn, w_up, conv_ffn_w, conv_ffn_b, w_down, g_final):
    bsz, s, _ = x.shape
    cos, sin = rope_tables(positions)
    c_act = jax.nn.silu(c)
    splits = [QK_WIDTH, 2 * QK_WIDTH, 2 * QK_WIDTH + ATTN_WIDTH,
              2 * QK_WIDTH + ATTN_WIDTH + RNN_WIDTH,
              2 * QK_WIDTH + ATTN_WIDTH + 2 * RNN_WIDTH,
              2 * QK_WIDTH + ATTN_WIDTH + 2 * RNN_WIDTH + D_MODEL]
    for l in range(DEPTH):
        lam_init = 0.8 - 0.6 * math.exp(-0.3 * l)
        mod = c_act @ w_mod[l] + b_mod[l]
        sh1, sc1, gt1, sh2, sc2, gt2 = [m[:, None, :] for m in jnp.split(mod, N_MOD, axis=-1)]

        h = rms_norm(x, g_norm_mix[l]) * (1.0 + sc1) + sh1
        proj = h @ w_in[l]
        q, k, v, xr, yr, ga, gr = jnp.split(proj, splits, axis=-1)
        q = apply_rope(q.reshape(bsz, s, N_HEADS, 2, HEAD_DIM), cos, sin)
        k = apply_rope(k.reshape(bsz, s, N_HEADS, 2, HEAD_DIM), cos, sin)
        v = v.reshape(bsz, s, N_HEADS, V_DIM)
        lam = (jnp.exp(jnp.sum(lam_q1[l] * lam_k1[l]).astype(jnp.float32))
               - jnp.exp(jnp.sum(lam_q2[l] * lam_k2[l]).astype(jnp.float32)) + lam_init)
        o_attn = diff_attention(q, k, v, lam, g_subln[l], 1.0 - lam_init)

        xr = causal_dwconv(xr, conv_rnn_w[l], conv_rnn_b[l])
        o_rnn = rg_lru(xr, w_rg_a[l], b_rg_a[l], w_rg_i[l], b_rg_i[l], lam_rg[l]) * jax.nn.gelu(yr)

        mixed = jax.nn.sigmoid(ga) * o_attn + jax.nn.sigmoid(gr) * o_rnn
        x = x + gt1 * (mixed @ w_out[l])

        h = rms_norm(x, g_norm_ffn[l]) * (1.0 + sc2) + sh2
        u = causal_dwconv(h @ w_up[l], conv_ffn_w[l], conv_ffn_b[l])
        val, gate = jnp.split(u, 2, axis=-1)
        x = x + gt2 * ((jax.nn.gelu(gate) * val) @ w_down[l])
    return rms_norm(x, g_final)
```

```python
import functools
import math

import jax
import jax.numpy as jnp
from jax import lax
from jax.experimental import pallas as pl
from jax.experimental.pallas import tpu as pltpu

F32 = jnp.float32
BF16 = jnp.bfloat16

D_MODEL = 1024
N_HEADS = 8
HEAD_DIM = 64
V_DIM = 128
RNN_BLOCKS = 8
RNN_BLOCK = 128
RNN_CONV = 4
RGLRU_C = 8.0
D_FF = 3 * D_MODEL
FFN_CONV = 3
ROPE_THETA = 10000.0
EPS = 1e-6
N_MOD = 6
IN_WIDTH = 7 * D_MODEL
CARRY_ROWS = 8

COL_Q, COL_K, COL_V, COL_XR, COL_YR, COL_GA, COL_GR = range(7)

TM_PROJ = 1024
TQ_ATTN = 512
TT_RNN = 256
TM_FFN = 512
FC_FFN = 1024
TM_ROPE = 2048
VMEM_LIMIT = 56 * 1024 * 1024


def _sigmoid(x):
    return 1.0 / (1.0 + jnp.exp(-x))


def _gelu_tanh(x):
    c = math.sqrt(2.0 / math.pi)
    return 0.5 * x * (1.0 + jnp.tanh(c * (x + 0.044715 * (x * x * x))))


def _rms_norm(x, g):
    ms = jnp.mean(x * x, axis=-1, keepdims=True)
    return x * lax.rsqrt(ms + EPS) * g


def _params(sem):
    return pltpu.CompilerParams(dimension_semantics=sem, vmem_limit_bytes=VMEM_LIMIT)


def _mod_kernel(c_ref, w_ref, b_ref, o_ref):
    c = c_ref[...]
    o_ref[...] = jnp.dot(c * _sigmoid(c), w_ref[...], preferred_element_type=F32) + b_ref[...]


def _modulation(c_pad, w_mod, b_mod):
    depth = w_mod.shape[0]
    n_col = w_mod.shape[2] // D_MODEL
    return pl.pallas_call(
        _mod_kernel,
        grid=(depth, n_col),
        in_specs=[
            pl.BlockSpec((CARRY_ROWS, D_MODEL), lambda l, n: (0, 0)),
            pl.BlockSpec((None, D_MODEL, D_MODEL), lambda l, n: (l, 0, n)),
            pl.BlockSpec((None, 1, D_MODEL), lambda l, n: (l, 0, n)),
        ],
        out_specs=pl.BlockSpec((None, CARRY_ROWS, D_MODEL), lambda l, n: (l, 0, n)),
        out_shape=jax.ShapeDtypeStruct((depth, CARRY_ROWS, N_MOD * D_MODEL), F32),
        compiler_params=_params(("arbitrary", "arbitrary")),
        name="modulation",
    )(c_pad, w_mod, b_mod.reshape(depth, 1, -1))


def _rope_kernel(pos_ref, invf_ref, cos_ref, sin_ref):
    ang = pos_ref[...].astype(F32) * invf_ref[...]
    lane = lax.broadcasted_iota(jnp.int32, ang.shape, 1)
    s = jnp.sin(ang)
    cos_ref[...] = jnp.cos(ang)
    sin_ref[...] = jnp.where((lane & (HEAD_DIM // 2)) == 0, -s, s)


def _rope_tables(pos_col, inv_freq_row):
    t = pos_col.shape[0]
    tm = min(TM_ROPE, t)
    return pl.pallas_call(
        _rope_kernel,
        grid=(t // tm,),
        in_specs=[pl.BlockSpec((tm, 1), lambda i: (i, 0)),
                  pl.BlockSpec((1, V_DIM), lambda i: (0, 0))],
        out_specs=[pl.BlockSpec((tm, V_DIM), lambda i: (i, 0)),
                   pl.BlockSpec((tm, V_DIM), lambda i: (i, 0))],
        out_shape=[jax.ShapeDtypeStruct((t, V_DIM), F32)] * 2,
        compiler_params=_params(("arbitrary",)),
        name="rope_tables",
    )(pos_col, inv_freq_row)


def _inproj_kernel(x_ref, g_ref, mod_ref, w_ref, cos_ref, sin_ref, o_ref, h_ref, *, tiles_per_batch):
    i = pl.program_id(0)
    n = pl.program_id(1)

    @pl.when(n == 0)
    def _():
        b = lax.div(i, tiles_per_batch)
        shift = mod_ref[pl.ds(b, 1), 0:D_MODEL]
        scale = mod_ref[pl.ds(b, 1), D_MODEL:2 * D_MODEL]
        y = _rms_norm(x_ref[...], g_ref[...])
        h_ref[...] = (y * (1.0 + scale) + shift).astype(BF16)

    acc = jnp.dot(h_ref[...], w_ref[...], preferred_element_type=F32)

    @pl.when(n <= COL_K)
    def _():
        qk_scale = jnp.where(n == COL_Q, HEAD_DIM ** -0.5, 1.0).astype(F32)
        cos = cos_ref[...] * qk_scale
        sin = sin_ref[...] * qk_scale
        lane = lax.broadcasted_iota(jnp.int32, cos.shape, 1)
        first_half = (lane & (HEAD_DIM // 2)) == 0
        for hh in range(N_HEADS):
            a = acc[:, hh * V_DIM:(hh + 1) * V_DIM]
            partner = jnp.where(first_half,
                                pltpu.roll(a, V_DIM - HEAD_DIM // 2, 1),
                                pltpu.roll(a, HEAD_DIM // 2, 1))
            o_ref[:, hh * V_DIM:(hh + 1) * V_DIM] = (a * cos + partner * sin).astype(BF16)

    @pl.when((n == COL_V) | (n == COL_XR))
    def _():
        o_ref[...] = acc.astype(BF16)

    @pl.when(n == COL_YR)
    def _():
        o_ref[...] = _gelu_tanh(acc).astype(BF16)

    @pl.when(n >= COL_GA)
    def _():
        o_ref[...] = _sigmoid(acc).astype(BF16)


def _in_projection(x2d, g_norm, mod_l, w_in_bf16, cos_t, sin_t, seq):
    t = x2d.shape[0]
    tm = min(TM_PROJ, seq)
    n_col = IN_WIDTH // D_MODEL
    kern = functools.partial(_inproj_kernel, tiles_per_batch=seq // tm)
    return pl.pallas_call(
        kern,
        grid=(t // tm, n_col),
        in_specs=[
            pl.BlockSpec((tm, D_MODEL), lambda i, n: (i, 0)),
            pl.BlockSpec((1, D_MODEL), lambda i, n: (0, 0)),
            pl.BlockSpec((CARRY_ROWS, N_MOD * D_MODEL), lambda i, n: (0, 0)),
            pl.BlockSpec((D_MODEL, D_MODEL), lambda i, n: (0, n)),
            pl.BlockSpec((tm, V_DIM), lambda i, n: (i, 0)),
            pl.BlockSpec((tm, V_DIM), lambda i, n: (i, 0)),
        ],
        out_specs=pl.BlockSpec((tm, D_MODEL), lambda i, n: (i, n)),
        out_shape=jax.ShapeDtypeStruct((t, IN_WIDTH), BF16),
        scratch_shapes=[pltpu.VMEM((tm, D_MODEL), BF16)],
        compiler_params=_params(("arbitrary", "arbitrary")),
        name="in_projection",
    )(x2d, g_norm, mod_l, w_in_bf16, cos_t, sin_t)


def _attn_kernel(q_ref, k_ref, v_ref, lq1_ref, lk1_ref, lq2_ref, lk2_ref, gs_ref, o_ref,
                 qm_ref, m_ref, l_ref, acc_ref, vt_ref, *, tq, n_chunks, lam_init):
    qi = pl.program_id(2)

    @pl.when(qi == 0)
    def _():
        for j in range(n_chunks):
            vt_ref[j] = v_ref[j * tq:(j + 1) * tq, :].astype(F32).T.astype(BF16)

    q = q_ref[...]
    lane = lax.broadcasted_iota(jnp.int32, q.shape, 1)
    zero = jnp.zeros_like(q)
    qm_ref[0] = jnp.where(lane < HEAD_DIM, q, zero)
    qm_ref[1] = jnp.where(lane >= HEAD_DIM, q, zero)
    m_ref[...] = jnp.full(m_ref.shape, -jnp.inf, F32)
    l_ref[...] = jnp.zeros(l_ref.shape, F32)
    acc_ref[...] = jnp.zeros(acc_ref.shape, F32)

    def process(j, diagonal):
        kc = k_ref[pl.ds(pl.multiple_of(j * tq, tq), tq), :]
        vt = vt_ref[j]
        for c in range(2):
            s_t = lax.dot_general(kc, qm_ref[c], (((1,), (1,)), ((), ())),
                                  preferred_element_type=F32)
            if diagonal:
                key = lax.broadcasted_iota(jnp.int32, s_t.shape, 0)
                qry = lax.broadcasted_iota(jnp.int32, s_t.shape, 1)
                s_t = jnp.where(key <= qry, s_t, -jnp.inf)
            m_old = m_ref[c]
            m_new = jnp.maximum(m_old, jnp.max(s_t, axis=0, keepdims=True))
            alpha = jnp.exp(m_old - m_new)
            p = jnp.exp(s_t - m_new)
            l_ref[c] = alpha * l_ref[c] + jnp.sum(p, axis=0, keepdims=True)
            acc_ref[c] = alpha * acc_ref[c] + jnp.dot(vt, p.astype(BF16),
                                                      preferred_element_type=F32)
            m_ref[c] = m_new

    def body(j, carry):
        process(j, False)
        return carry

    lax.fori_loop(0, qi, body, 0)
    process(qi, True)

    lam = (jnp.exp(jnp.sum(lq1_ref[...] * lk1_ref[...], axis=-1, keepdims=True))
           - jnp.exp(jnp.sum(lq2_ref[...] * lk2_ref[...], axis=-1, keepdims=True)) + lam_init)
    o_t = acc_ref[0] * (1.0 / l_ref[0]) - lam * (acc_ref[1] * (1.0 / l_ref[1]))
    o = o_t.T
    o_ref[...] = (_rms_norm(o, gs_ref[...]) * (1.0 - lam_init)).astype(o_ref.dtype)


def _attention(proj, lam_vecs, g_subln, layer, bsz, seq):
    t = proj.shape[0]
    tq = min(TQ_ATTN, seq)
    nq = seq // tq
    lam_init = 0.8 - 0.6 * math.exp(-0.3 * layer)
    kern = functools.partial(_attn_kernel, tq=tq, n_chunks=nq, lam_init=lam_init)
    hb = D_MODEL // V_DIM
    vec64 = pl.BlockSpec((1, HEAD_DIM), lambda b, h, i: (0, 0))
    return pl.pallas_call(
        kern,
        grid=(bsz, N_HEADS, nq),
        in_specs=[
            pl.BlockSpec((tq, V_DIM), lambda b, h, i: (b * nq + i, COL_Q * hb + h)),
            pl.BlockSpec((seq, V_DIM), lambda b, h, i: (b, COL_K * hb + h)),
            pl.BlockSpec((seq, V_DIM), lambda b, h, i: (b, COL_V * hb + h)),
            vec64, vec64, vec64, vec64,
            pl.BlockSpec((1, V_DIM), lambda b, h, i: (0, 0)),
        ],
        out_specs=pl.BlockSpec((tq, V_DIM), lambda b, h, i: (b * nq + i, h)),
        out_shape=jax.ShapeDtypeStruct((t, D_MODEL), BF16),
        scratch_shapes=[
            pltpu.VMEM((2, tq, V_DIM), BF16),
            pltpu.VMEM((2, 1, tq), F32),
            pltpu.VMEM((2, 1, tq), F32),
            pltpu.VMEM((2, V_DIM, tq), F32),
            pltpu.VMEM((nq, V_DIM, tq), BF16),
        ],
        compiler_params=_params(("arbitrary", "arbitrary", "arbitrary")),
        name="diff_attention",
    )(proj, proj, proj, *lam_vecs, g_subln)


def _rnn_kernel(xr_ref, gy_ref, sa_ref, sr_ref, oat_ref, cw_ref, cb_ref, wg_ref, ba_ref, bi_ref,
                lam_ref, o_ref, xbuf_ref, hprev_ref, *, tt):
    s = pl.program_id(1)

    @pl.when(s == 0)
    def _():
        xbuf_ref[0:CARRY_ROWS, :] = jnp.zeros((CARRY_ROWS, D_MODEL), F32)
        hprev_ref[...] = jnp.zeros(hprev_ref.shape, F32)

    xbuf_ref[CARRY_ROWS:CARRY_ROWS + tt, :] = xr_ref[...].astype(F32)
    xc = cb_ref[...]
    for j in range(RNN_CONV):
        off = CARRY_ROWS - (RNN_CONV - 1) + j
        xc = xc + cw_ref[j:j + 1, :] * xbuf_ref[off:off + tt, :]
    xbuf_ref[0:CARRY_ROWS, :] = xbuf_ref[tt:tt + CARRY_ROWS, :]

    lam = lam_ref[...]
    neg = -lam
    softplus = jnp.maximum(neg, 0.0) + jnp.log1p(jnp.exp(-jnp.abs(neg)))
    row = lax.broadcasted_iota(jnp.int32, (tt, RNN_BLOCK), 0)

    for nb in range(RNN_BLOCKS):
        cols = slice(nb * RNN_BLOCK, (nb + 1) * RNN_BLOCK)
        xb = xc[:, cols]
        g = jnp.dot(xb.astype(BF16), wg_ref[nb], preferred_element_type=F32)
        r = _sigmoid(g[:, :RNN_BLOCK] + ba_ref[:, cols])
        ig = _sigmoid(g[:, RNN_BLOCK:] + bi_ref[:, cols])
        log_a = (-RGLRU_C) * r * softplus[:, cols]
        a = jnp.exp(log_a)
        h = jnp.sqrt(-jnp.tanh(log_a) * (1.0 + a * a)) * (ig * xb)
        k = 1
        while k < tt:
            live = row >= k
            h = a * jnp.where(live, pltpu.roll(h, k, 0), 0.0) + h
            a = a * jnp.where(live, pltpu.roll(a, k, 0), 1.0)
            k *= 2
        h = h + a * hprev_ref[:, cols]
        hprev_ref[:, cols] = h[tt - 1:tt, :]
        o_rnn = h * gy_ref[:, cols].astype(F32)
        mixed = (sa_ref[:, cols].astype(F32) * oat_ref[:, cols].astype(F32)
                 + sr_ref[:, cols].astype(F32) * o_rnn)
        o_ref[:, cols] = mixed.astype(o_ref.dtype)


def _rnn_merge(proj, o_attn, conv_w, conv_b, w_gate_bf16, b_a, b_i, lam_rg, bsz, seq):
    t = proj.shape[0]
    tt = min(TT_RNN, seq)
    ns = seq // tt
    kern = functools.partial(_rnn_kernel, tt=tt)

    def col(cb):
        return pl.BlockSpec((tt, D_MODEL), lambda b, s: (b * ns + s, cb))

    row_vec = pl.BlockSpec((1, D_MODEL), lambda b, s: (0, 0))
    return pl.pallas_call(
        kern,
        grid=(bsz, ns),
        in_specs=[
            col(COL_XR), col(COL_YR), col(COL_GA), col(COL_GR),
            pl.BlockSpec((tt, D_MODEL), lambda b, s: (b * ns + s, 0)),
            pl.BlockSpec((RNN_CONV, D_MODEL), lambda b, s: (0, 0)),
            row_vec,
            pl.BlockSpec((RNN_BLOCKS, RNN_BLOCK, 2 * RNN_BLOCK), lambda b, s: (0, 0, 0)),
            row_vec, row_vec, row_vec,
        ],
        out_specs=pl.BlockSpec((tt, D_MODEL), lambda b, s: (b * ns + s, 0)),
        out_shape=jax.ShapeDtypeStruct((t, D_MODEL), BF16),
        scratch_shapes=[pltpu.VMEM((tt + CARRY_ROWS, D_MODEL), F32),
                        pltpu.VMEM((1, D_MODEL), F32)],
        compiler_params=_params(("arbitrary", "arbitrary")),
        name="rglru_merge",
    )(proj, proj, proj, proj, o_attn, conv_w, conv_b, w_gate_bf16, b_a, b_i, lam_rg)


def _ffn_kernel(mix_ref, x_ref, mod_ref, wo_ref, g_ref, wv_ref, wg_ref, cwv_ref, cwg_ref,
                cbv_ref, cbg_ref, wd_ref, gf_ref, o_ref,
                x1_ref, h_ref, acc_ref, ubv_ref, ubg_ref, *, tm, fc, tiles_per_batch, final):
    i = pl.program_id(0)
    j = pl.program_id(1)
    nj = pl.num_programs(1)
    b = lax.div(i, tiles_per_batch)
    first_tile = lax.rem(i, tiles_per_batch) == 0

    def mod(k):
        return mod_ref[pl.ds(b, 1), k * D_MODEL:(k + 1) * D_MODEL]

    @pl.when(j == 0)
    def _():
        attn = jnp.dot(mix_ref[...], wo_ref[...], preferred_element_type=F32)
        x1 = x_ref[...] + mod(2) * attn
        x1_ref[...] = x1
        h_ref[...] = (_rms_norm(x1, g_ref[...]) * (1.0 + mod(4)) + mod(3)).astype(BF16)
        acc_ref[...] = jnp.zeros(acc_ref.shape, F32)

    def conv_branch(w_ref, cw_ref, cb_ref, buf_ref):
        @pl.when(first_tile)
        def _():
            buf_ref[j, 0:CARRY_ROWS, :] = jnp.zeros((CARRY_ROWS, fc), F32)

        buf_ref[j, CARRY_ROWS:CARRY_ROWS + tm, :] = jnp.dot(
            h_ref[...], w_ref[...], preferred_element_type=F32)
        u = cb_ref[...]
        for tap in range(FFN_CONV):
            off = CARRY_ROWS - (FFN_CONV - 1) + tap
            u = u + cw_ref[tap:tap + 1, :] * buf_ref[j, off:off + tm, :]
        buf_ref[j, 0:CARRY_ROWS, :] = buf_ref[j, tm:tm + CARRY_ROWS, :]
        return u

    val = conv_branch(wv_ref, cwv_ref, cbv_ref, ubv_ref)
    gate = conv_branch(wg_ref, cwg_ref, cbg_ref, ubg_ref)
    hidden = (_gelu_tanh(gate) * val).astype(BF16)
    acc_ref[...] += jnp.dot(hidden, wd_ref[...], preferred_element_type=F32)

    @pl.when(j == nj - 1)
    def _():
        x2 = x1_ref[...] + mod(5) * acc_ref[...]
        if final:
            x2 = _rms_norm(x2, gf_ref[...])
        o_ref[...] = x2


def _out_ffn(mixed, x2d, mod_l, w_out_bf16, g_ffn, w_up_bf16, conv_w, conv_b, w_down_bf16,
             g_final, seq, final):
    t = x2d.shape[0]
    tm = min(TM_FFN, seq)
    fc = FC_FFN
    nj = D_FF // fc
    kern = functools.partial(_ffn_kernel, tm=tm, fc=fc, tiles_per_batch=seq // tm, final=final)
    row_vec = pl.BlockSpec((1, D_MODEL), lambda i, j: (0, 0))
    return pl.pallas_call(
        kern,
        grid=(t // tm, nj),
        in_specs=[
            pl.BlockSpec((tm, D_MODEL), lambda i, j: (i, 0)),
            pl.BlockSpec((tm, D_MODEL), lambda i, j: (i, 0)),
            pl.BlockSpec((CARRY_ROWS, N_MOD * D_MODEL), lambda i, j: (0, 0)),
            pl.BlockSpec((D_MODEL, D_MODEL), lambda i, j: (0, 0)),
            row_vec,
            pl.BlockSpec((D_MODEL, fc), lambda i, j: (0, j)),
            pl.BlockSpec((D_MODEL, fc), lambda i, j: (0, nj + j)),
            pl.BlockSpec((FFN_CONV, fc), lambda i, j: (0, j)),
            pl.BlockSpec((FFN_CONV, fc), lambda i, j: (0, nj + j)),
            pl.BlockSpec((1, fc), lambda i, j: (0, j)),
            pl.BlockSpec((1, fc), lambda i, j: (0, nj + j)),
            pl.BlockSpec((fc, D_MODEL), lambda i, j: (j, 0)),
            row_vec,
        ],
        out_specs=pl.BlockSpec((tm, D_MODEL), lambda i, j: (i, 0)),
        out_shape=jax.ShapeDtypeStruct((t, D_MODEL), F32),
        scratch_shapes=[
            pltpu.VMEM((tm, D_MODEL), F32),
            pltpu.VMEM((tm, D_MODEL), BF16),
            pltpu.VMEM((tm, D_MODEL), F32),
            pltpu.VMEM((nj, tm + CARRY_ROWS, fc), F32),
            pltpu.VMEM((nj, tm + CARRY_ROWS, fc), F32),
        ],
        compiler_params=_params(("arbitrary", "arbitrary")),
        name="outproj_ffn",
    )(mixed, x2d, mod_l, w_out_bf16, g_ffn, w_up_bf16, w_up_bf16, conv_w, conv_w, conv_b, conv_b,
      w_down_bf16, g_final)


def kernel(x, c, positions, w_mod, b_mod, g_norm_mix, w_in, lam_q1, lam_k1, lam_q2, lam_k2,
           g_subln, conv_rnn_w, conv_rnn_b, w_rg_a, b_rg_a, w_rg_i, b_rg_i, lam_rg,
           w_out, g_norm_ffn, w_up, conv_ffn_w, conv_ffn_b, w_down, g_final):
    bsz, seq, d = x.shape
    depth = w_mod.shape[0]
    assert d == D_MODEL and bsz <= CARRY_ROWS
    assert seq % max(TM_PROJ, TQ_ATTN, TT_RNN, TM_FFN) == 0
    t = bsz * seq

    c_pad = jnp.zeros((CARRY_ROWS, D_MODEL), F32).at[:bsz].set(c)
    mod = _modulation(c_pad, w_mod, b_mod)

    inv_freq = ROPE_THETA ** (-jnp.arange(0, HEAD_DIM, 2, dtype=F32) / HEAD_DIM)
    inv_freq_row = jnp.tile(inv_freq, V_DIM // (HEAD_DIM // 2)).reshape(1, V_DIM)
    cos_t, sin_t = _rope_tables(positions.reshape(t, 1), inv_freq_row)

    x2d = x.reshape(t, D_MODEL)
    for l in range(depth):
        proj = _in_projection(x2d, g_norm_mix[l].reshape(1, -1), mod[l], w_in[l].astype(BF16),
                              cos_t, sin_t, seq)
        lam_vecs = [v[l].reshape(1, HEAD_DIM) for v in (lam_q1, lam_k1, lam_q2, lam_k2)]
        o_attn = _attention(proj, lam_vecs, g_subln[l].reshape(1, V_DIM), l, bsz, seq)
        w_gate = jnp.concatenate([w_rg_a[l], w_rg_i[l]], axis=-1).astype(BF16)
        mixed = _rnn_merge(proj, o_attn, conv_rnn_w[l], conv_rnn_b[l].reshape(1, -1), w_gate,
                           b_rg_a[l].reshape(1, -1), b_rg_i[l].reshape(1, -1),
                           lam_rg[l].reshape(1, -1), bsz, seq)
        x2d = _out_ffn(mixed, x2d, mod[l], w_out[l].astype(BF16), g_norm_ffn[l].reshape(1, -1),
                       w_up[l].astype(BF16), conv_ffn_w[l], conv_ffn_b[l].reshape(1, -1),
                       w_down[l].astype(BF16), g_final.reshape(1, -1), seq, l == depth - 1)
    return x2d.reshape(bsz, seq, D_MODEL)
```

```python
import functools
import math

import jax
import jax.numpy as jnp
from jax import lax
from jax.experimental import pallas as pl
from jax.experimental.pallas import tpu as pltpu

F32 = jnp.float32
BF16 = jnp.bfloat16

D_MODEL = 1024
N_HEADS = 8
HEAD_DIM = 64
V_DIM = 128
RNN_BLOCKS = 8
RNN_BLOCK = 128
RNN_CONV = 4
RGLRU_C = 8.0
D_FF = 3 * D_MODEL
FFN_CONV = 3
ROPE_THETA = 10000.0
EPS = 1e-6
LOG2_E = math.log2(math.e)
N_MOD = 6
IN_WIDTH = 7 * D_MODEL
CARRY_ROWS = 8

COL_Q, COL_K, COL_V, COL_XR, COL_YR, COL_GA, COL_GR = range(7)

TM_PROJ = 1024
TQ_ATTN = 512
TT_RNN = 256
TM_FFN = 512
FC_FFN = 1024
TM_ROPE = 2048
VMEM_LIMIT = 56 * 1024 * 1024


def _sigmoid(x):
    return 1.0 / (1.0 + jnp.exp(-x))


def _gelu_tanh(x):
    c = math.sqrt(2.0 / math.pi)
    return 0.5 * x * (1.0 + jnp.tanh(c * (x + 0.044715 * (x * x * x))))


def _rms_norm(x, g):
    ms = jnp.mean(x * x, axis=-1, keepdims=True)
    return x * lax.rsqrt(ms + EPS) * g


def _params(sem):
    return pltpu.CompilerParams(dimension_semantics=sem, vmem_limit_bytes=VMEM_LIMIT)


def _mod_kernel(c_ref, w_ref, b_ref, o_ref):
    c = c_ref[...]
    o_ref[...] = jnp.dot(c * _sigmoid(c), w_ref[...], preferred_element_type=F32) + b_ref[...]


def _modulation(c_pad, w_mod, b_mod):
    depth = w_mod.shape[0]
    n_col = w_mod.shape[2] // D_MODEL
    return pl.pallas_call(
        _mod_kernel,
        grid=(depth, n_col),
        in_specs=[
            pl.BlockSpec((CARRY_ROWS, D_MODEL), lambda l, n: (0, 0)),
            pl.BlockSpec((None, D_MODEL, D_MODEL), lambda l, n: (l, 0, n)),
            pl.BlockSpec((None, 1, D_MODEL), lambda l, n: (l, 0, n)),
        ],
        out_specs=pl.BlockSpec((None, CARRY_ROWS, D_MODEL), lambda l, n: (l, 0, n)),
        out_shape=jax.ShapeDtypeStruct((depth, CARRY_ROWS, N_MOD * D_MODEL), F32),
        compiler_params=_params(("arbitrary", "arbitrary")),
        name="modulation",
    )(c_pad, w_mod, b_mod.reshape(depth, 1, -1))


def _rope_kernel(pos_ref, invf_ref, cos_ref, sin_ref):
    ang = pos_ref[...].astype(F32) * invf_ref[...]
    lane = lax.broadcasted_iota(jnp.int32, ang.shape, 1)
    s = jnp.sin(ang)
    cos_ref[...] = jnp.cos(ang)
    sin_ref[...] = jnp.where((lane & (HEAD_DIM // 2)) == 0, -s, s)


def _rope_tables(pos_col, inv_freq_row):
    t = pos_col.shape[0]
    tm = min(TM_ROPE, t)
    return pl.pallas_call(
        _rope_kernel,
        grid=(t // tm,),
        in_specs=[pl.BlockSpec((tm, 1), lambda i: (i, 0)),
                  pl.BlockSpec((1, V_DIM), lambda i: (0, 0))],
        out_specs=[pl.BlockSpec((tm, V_DIM), lambda i: (i, 0)),
                   pl.BlockSpec((tm, V_DIM), lambda i: (i, 0))],
        out_shape=[jax.ShapeDtypeStruct((t, V_DIM), F32)] * 2,
        compiler_params=_params(("arbitrary",)),
        name="rope_tables",
    )(pos_col, inv_freq_row)


def _inproj_kernel(x_ref, g_ref, mod_ref, w_ref, cos_ref, sin_ref, o_ref, h_ref, *, tiles_per_batch):
    i = pl.program_id(0)
    n = pl.program_id(1)

    @pl.when(n == 0)
    def _():
        b = lax.div(i, tiles_per_batch)
        shift = mod_ref[pl.ds(b, 1), 0:D_MODEL]
        scale = mod_ref[pl.ds(b, 1), D_MODEL:2 * D_MODEL]
        y = _rms_norm(x_ref[...], g_ref[...])
        h_ref[...] = (y * (1.0 + scale) + shift).astype(BF16)

    acc = jnp.dot(h_ref[...], w_ref[...], preferred_element_type=F32)

    @pl.when(n <= COL_K)
    def _():
        qk_scale = jnp.where(n == COL_Q, LOG2_E * HEAD_DIM ** -0.5, 1.0).astype(F32)
        cos = cos_ref[...] * qk_scale
        sin = sin_ref[...] * qk_scale
        lane = lax.broadcasted_iota(jnp.int32, cos.shape, 1)
        first_half = (lane & (HEAD_DIM // 2)) == 0
        for hh in range(N_HEADS):
            a = acc[:, hh * V_DIM:(hh + 1) * V_DIM]
            partner = jnp.where(first_half,
                                pltpu.roll(a, V_DIM - HEAD_DIM // 2, 1),
                                pltpu.roll(a, HEAD_DIM // 2, 1))
            o_ref[:, hh * V_DIM:(hh + 1) * V_DIM] = (a * cos + partner * sin).astype(BF16)

    @pl.when((n == COL_V) | (n == COL_XR))
    def _():
        o_ref[...] = acc.astype(BF16)

    @pl.when(n == COL_YR)
    def _():
        o_ref[...] = _gelu_tanh(acc).astype(BF16)

    @pl.when(n >= COL_GA)
    def _():
        o_ref[...] = _sigmoid(acc).astype(BF16)


def _in_projection(x2d, g_norm, mod_l, w_in_bf16, cos_t, sin_t, seq):
    t = x2d.shape[0]
    tm = min(TM_PROJ, seq)
    n_col = IN_WIDTH // D_MODEL
    kern = functools.partial(_inproj_kernel, tiles_per_batch=seq // tm)
    return pl.pallas_call(
        kern,
        grid=(t // tm, n_col),
        in_specs=[
            pl.BlockSpec((tm, D_MODEL), lambda i, n: (i, 0)),
            pl.BlockSpec((1, D_MODEL), lambda i, n: (0, 0)),
            pl.BlockSpec((CARRY_ROWS, N_MOD * D_MODEL), lambda i, n: (0, 0)),
            pl.BlockSpec((D_MODEL, D_MODEL), lambda i, n: (0, n)),
            pl.BlockSpec((tm, V_DIM), lambda i, n: (i, 0)),
            pl.BlockSpec((tm, V_DIM), lambda i, n: (i, 0)),
        ],
        out_specs=pl.BlockSpec((tm, D_MODEL), lambda i, n: (i, n)),
        out_shape=jax.ShapeDtypeStruct((t, IN_WIDTH), BF16),
        scratch_shapes=[pltpu.VMEM((tm, D_MODEL), BF16)],
        compiler_params=_params(("arbitrary", "arbitrary")),
        name="in_projection",
    )(x2d, g_norm, mod_l, w_in_bf16, cos_t, sin_t)


def _attn_kernel(q_ref, k_ref, v_ref, lq1_ref, lk1_ref, lq2_ref, lk2_ref, gs_ref, o_ref,
                 qm_ref, m_ref, l_ref, acc_ref, vt_ref, sa_ref, sb_ref, *, tq, n_chunks,
                 lam_init):
    qi = pl.program_id(2)

    @pl.when(qi == 0)
    def _():
        for j in range(n_chunks):
            vt_ref[j] = v_ref[j * tq:(j + 1) * tq, :].astype(F32).T.astype(BF16)

    q = q_ref[...]
    lane = lax.broadcasted_iota(jnp.int32, q.shape, 1)
    zero = jnp.zeros_like(q)
    qm_ref[0] = jnp.where(lane < HEAD_DIM, q, zero)
    qm_ref[1] = jnp.where(lane >= HEAD_DIM, q, zero)
    m_ref[...] = jnp.full(m_ref.shape, -jnp.inf, F32)
    l_ref[...] = jnp.zeros(l_ref.shape, F32)
    acc_ref[...] = jnp.zeros(acc_ref.shape, F32)

    def scores(j, s_ref):
        kc = k_ref[pl.ds(pl.multiple_of(j * tq, tq), tq), :]
        for c in range(2):
            s_ref[c] = lax.dot_general(kc, qm_ref[c], (((1,), (1,)), ((), ())),
                                       preferred_element_type=F32)

    def softmax_pv(j, s_ref, diagonal):
        vt = vt_ref[j]
        for c in range(2):
            s_t = s_ref[c]
            if diagonal:
                key = lax.broadcasted_iota(jnp.int32, s_t.shape, 0)
                qry = lax.broadcasted_iota(jnp.int32, s_t.shape, 1)
                s_t = jnp.where(key <= qry, s_t, -jnp.inf)
            m_old = m_ref[c]
            m_new = jnp.maximum(m_old, jnp.max(s_t, axis=0, keepdims=True))
            alpha = jnp.exp2(m_old - m_new)
            p = jnp.exp2(s_t - m_new)
            l_ref[c] = alpha * l_ref[c] + jnp.sum(p, axis=0, keepdims=True)
            acc_ref[c] = alpha * acc_ref[c] + jnp.dot(vt, p.astype(BF16),
                                                      preferred_element_type=F32)
            m_ref[c] = m_new

    scores(0, sa_ref)

    def pair(jj, carry):
        j = 2 * jj
        scores(j + 1, sb_ref)
        softmax_pv(j, sa_ref, False)
        scores(j + 2, sa_ref)
        softmax_pv(j + 1, sb_ref, False)
        return carry

    lax.fori_loop(0, lax.div(qi, 2), pair, 0)

    @pl.when(lax.rem(qi, 2) == 0)
    def _():
        softmax_pv(qi, sa_ref, True)

    @pl.when(lax.rem(qi, 2) == 1)
    def _():
        scores(qi, sb_ref)
        softmax_pv(qi - 1, sa_ref, False)
        softmax_pv(qi, sb_ref, True)

    lam = (jnp.exp(jnp.sum(lq1_ref[...] * lk1_ref[...], axis=-1, keepdims=True))
           - jnp.exp(jnp.sum(lq2_ref[...] * lk2_ref[...], axis=-1, keepdims=True)) + lam_init)
    o_t = acc_ref[0] * (1.0 / l_ref[0]) - lam * (acc_ref[1] * (1.0 / l_ref[1]))
    o = o_t.T
    o_ref[...] = (_rms_norm(o, gs_ref[...]) * (1.0 - lam_init)).astype(o_ref.dtype)


def _attention(proj, lam_vecs, g_subln, layer, bsz, seq):
    t = proj.shape[0]
    tq = min(TQ_ATTN, seq)
    nq = seq // tq
    lam_init = 0.8 - 0.6 * math.exp(-0.3 * layer)
    kern = functools.partial(_attn_kernel, tq=tq, n_chunks=nq, lam_init=lam_init)
    hb = D_MODEL // V_DIM
    vec64 = pl.BlockSpec((1, HEAD_DIM), lambda b, h, i: (0, 0))
    return pl.pallas_call(
        kern,
        grid=(bsz, N_HEADS, nq),
        in_specs=[
            pl.BlockSpec((tq, V_DIM), lambda b, h, i: (b * nq + i, COL_Q * hb + h)),
            pl.BlockSpec((seq, V_DIM), lambda b, h, i: (b, COL_K * hb + h)),
            pl.BlockSpec((seq, V_DIM), lambda b, h, i: (b, COL_V * hb + h)),
            vec64, vec64, vec64, vec64,
            pl.BlockSpec((1, V_DIM), lambda b, h, i: (0, 0)),
        ],
        out_specs=pl.BlockSpec((tq, V_DIM), lambda b, h, i: (b * nq + i, h)),
        out_shape=jax.ShapeDtypeStruct((t, D_MODEL), BF16),
        scratch_shapes=[
            pltpu.VMEM((2, tq, V_DIM), BF16),
            pltpu.VMEM((2, 1, tq), F32),
            pltpu.VMEM((2, 1, tq), F32),
            pltpu.VMEM((2, V_DIM, tq), F32),
            pltpu.VMEM((nq, V_DIM, tq), BF16),
            pltpu.VMEM((2, tq, tq), F32),
            pltpu.VMEM((2, tq, tq), F32),
        ],
        compiler_params=_params(("arbitrary", "arbitrary", "arbitrary")),
        name="diff_attention",
    )(proj, proj, proj, *lam_vecs, g_subln)


def _rnn_kernel(xr_ref, gy_ref, sa_ref, sr_ref, oat_ref, cw_ref, cb_ref, wg_ref, ba_ref, bi_ref,
                lam_ref, o_ref, xbuf_ref, hprev_ref, *, tt):
    s = pl.program_id(1)

    @pl.when(s == 0)
    def _():
        xbuf_ref[0:CARRY_ROWS, :] = jnp.zeros((CARRY_ROWS, D_MODEL), F32)
        hprev_ref[...] = jnp.zeros(hprev_ref.shape, F32)

    xbuf_ref[CARRY_ROWS:CARRY_ROWS + tt, :] = xr_ref[...].astype(F32)
    xc = cb_ref[...]
    for j in range(RNN_CONV):
        off = CARRY_ROWS - (RNN_CONV - 1) + j
        xc = xc + cw_ref[j:j + 1, :] * xbuf_ref[off:off + tt, :]
    xbuf_ref[0:CARRY_ROWS, :] = xbuf_ref[tt:tt + CARRY_ROWS, :]

    lam = lam_ref[...]
    neg = -lam
    softplus = jnp.maximum(neg, 0.0) + jnp.log1p(jnp.exp(-jnp.abs(neg)))
    row = lax.broadcasted_iota(jnp.int32, (tt, RNN_BLOCK), 0)

    for nb in range(RNN_BLOCKS):
        cols = slice(nb * RNN_BLOCK, (nb + 1) * RNN_BLOCK)
        xb = xc[:, cols]
        g = jnp.dot(xb.astype(BF16), wg_ref[nb], preferred_element_type=F32)
        r = _sigmoid(g[:, :RNN_BLOCK] + ba_ref[:, cols])
        ig = _sigmoid(g[:, RNN_BLOCK:] + bi_ref[:, cols])
        log_a = (-RGLRU_C) * r * softplus[:, cols]
        a = jnp.exp(log_a)
        h = jnp.sqrt(-jnp.tanh(log_a) * (1.0 + a * a)) * (ig * xb)
        k = 1
        while k < tt:
            live = row >= k
            h = a * jnp.where(live, pltpu.roll(h, k, 0), 0.0) + h
            a = a * jnp.where(live, pltpu.roll(a, k, 0), 1.0)
            k *= 2
        h = h + a * hprev_ref[:, cols]
        hprev_ref[:, cols] = h[tt - 1:tt, :]
        o_rnn = h * gy_ref[:, cols].astype(F32)
        mixed = (sa_ref[:, cols].astype(F32) * oat_ref[:, cols].astype(F32)
                 + sr_ref[:, cols].astype(F32) * o_rnn)
        o_ref[:, cols] = mixed.astype(o_ref.dtype)


def _rnn_merge(proj, o_attn, conv_w, conv_b, w_gate_bf16, b_a, b_i, lam_rg, bsz, seq):
    t = proj.shape[0]
    tt = min(TT_RNN, seq)
    ns = seq // tt
    kern = functools.partial(_rnn_kernel, tt=tt)

    def col(cb):
        return pl.BlockSpec((tt, D_MODEL), lambda b, s: (b * ns + s, cb))

    row_vec = pl.BlockSpec((1, D_MODEL), lambda b, s: (0, 0))
    return pl.pallas_call(
        kern,
        grid=(bsz, ns),
        in_specs=[
            col(COL_XR), col(COL_YR), col(COL_GA), col(COL_GR),
            pl.BlockSpec((tt, D_MODEL), lambda b, s: (b * ns + s, 0)),
            pl.BlockSpec((RNN_CONV, D_MODEL), lambda b, s: (0, 0)),
            row_vec,
            pl.BlockSpec((RNN_BLOCKS, RNN_BLOCK, 2 * RNN_BLOCK), lambda b, s: (0, 0, 0)),
            row_vec, row_vec, row_vec,
        ],
        out_specs=pl.BlockSpec((tt, D_MODEL), lambda b, s: (b * ns + s, 0)),
        out_shape=jax.ShapeDtypeStruct((t, D_MODEL), BF16),
        scratch_shapes=[pltpu.VMEM((tt + CARRY_ROWS, D_MODEL), F32),
                        pltpu.VMEM((1, D_MODEL), F32)],
        compiler_params=_params(("arbitrary", "arbitrary")),
        name="rglru_merge",
    )(proj, proj, proj, proj, o_attn, conv_w, conv_b, w_gate_bf16, b_a, b_i, lam_rg)


def _ffn_kernel(mix_ref, x_ref, mod_ref, wo_ref, g_ref, wv_ref, wg_ref, cwv_ref, cwg_ref,
                cbv_ref, cbg_ref, wd_ref, gf_ref, o_ref,
                x1_ref, h_ref, acc_ref, ubv_ref, ubg_ref, *, tm, fc, tiles_per_batch, final):
    i = pl.program_id(0)
    j = pl.program_id(1)
    nj = pl.num_programs(1)
    b = lax.div(i, tiles_per_batch)
    first_tile = lax.rem(i, tiles_per_batch) == 0

    def mod(k):
        return mod_ref[pl.ds(b, 1), k * D_MODEL:(k + 1) * D_MODEL]

    @pl.when(j == 0)
    def _():
        attn = jnp.dot(mix_ref[...], wo_ref[...], preferred_element_type=F32)
        x1 = x_ref[...] + mod(2) * attn
        x1_ref[...] = x1
        h_ref[...] = (_rms_norm(x1, g_ref[...]) * (1.0 + mod(4)) + mod(3)).astype(BF16)
        acc_ref[...] = jnp.zeros(acc_ref.shape, F32)

    def conv_branch(w_ref, cw_ref, cb_ref, buf_ref):
        @pl.when(first_tile)
        def _():
            buf_ref[j, 0:CARRY_ROWS, :] = jnp.zeros((CARRY_ROWS, fc), F32)

        buf_ref[j, CARRY_ROWS:CARRY_ROWS + tm, :] = jnp.dot(
            h_ref[...], w_ref[...], preferred_element_type=F32)
        u = cb_ref[...]
        for tap in range(FFN_CONV):
            off = CARRY_ROWS - (FFN_CONV - 1) + tap
            u = u + cw_ref[tap:tap + 1, :] * buf_ref[j, off:off + tm, :]
        buf_ref[j, 0:CARRY_ROWS, :] = buf_ref[j, tm:tm + CARRY_ROWS, :]
        return u

    val = conv_branch(wv_ref, cwv_ref, cbv_ref, ubv_ref)
    gate = conv_branch(wg_ref, cwg_ref, cbg_ref, ubg_ref)
    hidden = (_gelu_tanh(gate) * val).astype(BF16)
    acc_ref[...] += jnp.dot(hidden, wd_ref[...], preferred_element_type=F32)

    @pl.when(j == nj - 1)
    def _():
        x2 = x1_ref[...] + mod(5) * acc_ref[...]
        if final:
            x2 = _rms_norm(x2, gf_ref[...])
        o_ref[...] = x2


def _out_ffn(mixed, x2d, mod_l, w_out_bf16, g_ffn, w_up_bf16, conv_w, conv_b, w_down_bf16,
             g_final, seq, final):
    t = x2d.shape[0]
    tm = min(TM_FFN, seq)
    fc = FC_FFN
    nj = D_FF // fc
    kern = functools.partial(_ffn_kernel, tm=tm, fc=fc, tiles_per_batch=seq // tm, final=final)
    row_vec = pl.BlockSpec((1, D_MODEL), lambda i, j: (0, 0))
    return pl.pallas_call(
        kern,
        grid=(t // tm, nj),
        in_specs=[
            pl.BlockSpec((tm, D_MODEL), lambda i, j: (i, 0)),
            pl.BlockSpec((tm, D_MODEL), lambda i, j: (i, 0)),
            pl.BlockSpec((CARRY_ROWS, N_MOD * D_MODEL), lambda i, j: (0, 0)),
            pl.BlockSpec((D_MODEL, D_MODEL), lambda i, j: (0, 0)),
            row_vec,
            pl.BlockSpec((D_MODEL, fc), lambda i, j: (0, j)),
            pl.BlockSpec((D_MODEL, fc), lambda i, j: (0, nj + j)),
            pl.BlockSpec((FFN_CONV, fc), lambda i, j: (0, j)),
            pl.BlockSpec((FFN_CONV, fc), lambda i, j: (0, nj + j)),
            pl.BlockSpec((1, fc), lambda i, j: (0, j)),
            pl.BlockSpec((1, fc), lambda i, j: (0, nj + j)),
            pl.BlockSpec((fc, D_MODEL), lambda i, j: (j, 0)),
            row_vec,
        ],
        out_specs=pl.BlockSpec((tm, D_MODEL), lambda i, j: (i, 0)),
        out_shape=jax.ShapeDtypeStruct((t, D_MODEL), F32),
        scratch_shapes=[
            pltpu.VMEM((tm, D_MODEL), F32),
            pltpu.VMEM((tm, D_MODEL), BF16),
            pltpu.VMEM((tm, D_MODEL), F32),
            pltpu.VMEM((nj, tm + CARRY_ROWS, fc), F32),
            pltpu.VMEM((nj, tm + CARRY_ROWS, fc), F32),
        ],
        compiler_params=_params(("arbitrary", "arbitrary")),
        name="outproj_ffn",
    )(mixed, x2d, mod_l, w_out_bf16, g_ffn, w_up_bf16, w_up_bf16, conv_w, conv_w, conv_b, conv_b,
      w_down_bf16, g_final)


def kernel(x, c, positions, w_mod, b_mod, g_norm_mix, w_in, lam_q1, lam_k1, lam_q2, lam_k2,
           g_subln, conv_rnn_w, conv_rnn_b, w_rg_a, b_rg_a, w_rg_i, b_rg_i, lam_rg,
           w_out, g_norm_ffn, w_up, conv_ffn_w, conv_ffn_b, w_down, g_final):
    bsz, seq, d = x.shape
    depth = w_mod.shape[0]
    assert d == D_MODEL and bsz <= CARRY_ROWS
    assert seq % max(TM_PROJ, TQ_ATTN, TT_RNN, TM_FFN) == 0
    t = bsz * seq

    c_pad = jnp.zeros((CARRY_ROWS, D_MODEL), F32).at[:bsz].set(c)
    mod = _modulation(c_pad, w_mod, b_mod)

    inv_freq = ROPE_THETA ** (-jnp.arange(0, HEAD_DIM, 2, dtype=F32) / HEAD_DIM)
    inv_freq_row = jnp.tile(inv_freq, V_DIM // (HEAD_DIM // 2)).reshape(1, V_DIM)
    cos_t, sin_t = _rope_tables(positions.reshape(t, 1), inv_freq_row)

    x2d = x.reshape(t, D_MODEL)
    for l in range(depth):
        proj = _in_projection(x2d, g_norm_mix[l].reshape(1, -1), mod[l], w_in[l].astype(BF16),
                              cos_t, sin_t, seq)
        lam_vecs = [v[l].reshape(1, HEAD_DIM) for v in (lam_q1, lam_k1, lam_q2, lam_k2)]
        o_attn = _attention(proj, lam_vecs, g_subln[l].reshape(1, V_DIM), l, bsz, seq)
        w_gate = jnp.concatenate([w_rg_a[l], w_rg_i[l]], axis=-1).astype(BF16)
        mixed = _rnn_merge(proj, o_attn, conv_rnn_w[l], conv_rnn_b[l].reshape(1, -1), w_gate,
                           b_rg_a[l].reshape(1, -1), b_rg_i[l].reshape(1, -1),
                           lam_rg[l].reshape(1, -1), bsz, seq)
        x2d = _out_ffn(mixed, x2d, mod[l], w_out[l].astype(BF16), g_norm_ffn[l].reshape(1, -1),
                       w_up[l].astype(BF16), conv_ffn_w[l], conv_ffn_b[l].reshape(1, -1),
                       w_down[l].astype(BF16), g_final.reshape(1, -1), seq, l == depth - 1)
    return x2d.reshape(bsz, seq, D_MODEL)
```

```python
import functools
import math

import jax
import jax.numpy as jnp
from jax import lax
from jax.experimental import pallas as pl
from jax.experimental.pallas import tpu as pltpu

F32 = jnp.float32
BF16 = jnp.bfloat16

D_MODEL = 1024
N_HEADS = 8
HEAD_DIM = 64
V_DIM = 128
RNN_BLOCKS = 8
RNN_BLOCK = 128
RNN_CONV = 4
RGLRU_C = 8.0
D_FF = 3 * D_MODEL
FFN_CONV = 3
ROPE_THETA = 10000.0
EPS = 1e-6
LOG2_E = math.log2(math.e)
N_MOD = 6
IN_WIDTH = 7 * D_MODEL
CARRY_ROWS = 8

COL_Q, COL_K, COL_V, COL_XR, COL_YR, COL_GA, COL_GR = range(7)

TM_PROJ = 512
NB_PROJ = 256
TQ_ATTN = 512
TT_RNN = 256
TM_FFN = 512
FC_FFN = 1024
TM_ROPE = 2048
VMEM_LIMIT = 56 * 1024 * 1024


def _sigmoid(x):
    return 1.0 / (1.0 + jnp.exp(-x))


def _gelu_tanh(x):
    c = math.sqrt(2.0 / math.pi)
    return 0.5 * x * (1.0 + jnp.tanh(c * (x + 0.044715 * (x * x * x))))


def _rms_norm(x, g):
    ms = jnp.mean(x * x, axis=-1, keepdims=True)
    return x * lax.rsqrt(ms + EPS) * g


def _shift_rows(x3, prev, k, sub):
    r = pltpu.roll(x3, k, 1)
    before = jnp.concatenate([pltpu.roll(prev, k, 0)[None], r[:-1]], axis=0)
    return jnp.where(sub < k, before, r)


def _params(sem):
    return pltpu.CompilerParams(dimension_semantics=sem, vmem_limit_bytes=VMEM_LIMIT)


def _resident(shape):
    return pl.BlockSpec(shape, lambda *_: (0,) * len(shape), pipeline_mode=pl.Buffered(1))


def _mod_kernel(c_ref, w_ref, b_ref, o_ref):
    c = c_ref[...]
    o_ref[...] = jnp.dot(c * _sigmoid(c), w_ref[...], preferred_element_type=F32) + b_ref[...]


def _modulation(c_pad, w_mod, b_mod):
    depth = w_mod.shape[0]
    n_col = w_mod.shape[2] // D_MODEL
    return pl.pallas_call(
        _mod_kernel,
        grid=(depth, n_col),
        in_specs=[
            pl.BlockSpec((CARRY_ROWS, D_MODEL), lambda l, n: (0, 0)),
            pl.BlockSpec((None, D_MODEL, D_MODEL), lambda l, n: (l, 0, n)),
            pl.BlockSpec((None, 1, D_MODEL), lambda l, n: (l, 0, n)),
        ],
        out_specs=pl.BlockSpec((None, CARRY_ROWS, D_MODEL), lambda l, n: (l, 0, n)),
        out_shape=jax.ShapeDtypeStruct((depth, CARRY_ROWS, N_MOD * D_MODEL), F32),
        compiler_params=_params(("arbitrary", "arbitrary")),
        name="modulation",
    )(c_pad, w_mod, b_mod.reshape(depth, 1, -1))


def _rope_kernel(pos_ref, invf_ref, cos_ref, sin_ref):
    ang = pos_ref[...].astype(F32) * invf_ref[...]
    lane = lax.broadcasted_iota(jnp.int32, ang.shape, 1)
    s = jnp.sin(ang)
    cos_ref[...] = jnp.cos(ang)
    sin_ref[...] = jnp.where((lane & (HEAD_DIM // 2)) == 0, -s, s)


def _rope_tables(pos_col, inv_freq_row):
    t = pos_col.shape[0]
    tm = min(TM_ROPE, t)
    return pl.pallas_call(
        _rope_kernel,
        grid=(t // tm,),
        in_specs=[pl.BlockSpec((tm, 1), lambda i: (i, 0)),
                  pl.BlockSpec((1, V_DIM), lambda i: (0, 0))],
        out_specs=[pl.BlockSpec((tm, V_DIM), lambda i: (i, 0)),
                   pl.BlockSpec((tm, V_DIM), lambda i: (i, 0))],
        out_shape=[jax.ShapeDtypeStruct((t, V_DIM), F32)] * 2,
        compiler_params=_params(("arbitrary",)),
        name="rope_tables",
    )(pos_col, inv_freq_row)


def _inproj_kernel(x_ref, g_ref, mod_ref, w_ref, cos_ref, sin_ref, o_ref, *, tiles_per_batch, nb):
    b = lax.div(pl.program_id(0), tiles_per_batch)
    shift = mod_ref[pl.ds(b, 1), 0:D_MODEL]
    scale = mod_ref[pl.ds(b, 1), D_MODEL:2 * D_MODEL]
    h = (_rms_norm(x_ref[...], g_ref[...]) * (1.0 + scale) + shift).astype(BF16)

    cos = cos_ref[...]
    sin = sin_ref[...]
    lane = lax.broadcasted_iota(jnp.int32, cos.shape, 1)
    first_half = (lane & (HEAD_DIM // 2)) == 0
    q_scale = LOG2_E * HEAD_DIM ** -0.5

    for blk in range(IN_WIDTH // nb):
        cols = slice(blk * nb, (blk + 1) * nb)
        kind = (blk * nb) // D_MODEL
        acc = jnp.dot(h, w_ref[:, cols], preferred_element_type=F32)
        if kind in (COL_Q, COL_K):
            for g in range(nb // V_DIM):
                a = acc[:, g * V_DIM:(g + 1) * V_DIM]
                partner = jnp.where(first_half,
                                    pltpu.roll(a, V_DIM - HEAD_DIM // 2, 1),
                                    pltpu.roll(a, HEAD_DIM // 2, 1))
                r = a * cos + partner * sin
                if kind == COL_Q:
                    r = r * q_scale
                o_ref[:, blk * nb + g * V_DIM:blk * nb + (g + 1) * V_DIM] = r.astype(BF16)
        elif kind in (COL_V, COL_XR):
            o_ref[:, cols] = acc.astype(BF16)
        elif kind == COL_YR:
            o_ref[:, cols] = _gelu_tanh(acc).astype(BF16)
        else:
            o_ref[:, cols] = _sigmoid(acc).astype(BF16)


def _in_projection(x2d, g_norm, mod_l, w_in_bf16, cos_t, sin_t, seq):
    t = x2d.shape[0]
    tm = min(TM_PROJ, seq)
    kern = functools.partial(_inproj_kernel, tiles_per_batch=seq // tm, nb=NB_PROJ)
    return pl.pallas_call(
        kern,
        grid=(t // tm,),
        in_specs=[
            pl.BlockSpec((tm, D_MODEL), lambda i: (i, 0)),
            pl.BlockSpec((1, D_MODEL), lambda i: (0, 0)),
            pl.BlockSpec((CARRY_ROWS, N_MOD * D_MODEL), lambda i: (0, 0)),
            _resident((D_MODEL, IN_WIDTH)),
            pl.BlockSpec((tm, V_DIM), lambda i: (i, 0)),
            pl.BlockSpec((tm, V_DIM), lambda i: (i, 0)),
        ],
        out_specs=pl.BlockSpec((tm, IN_WIDTH), lambda i: (i, 0)),
        out_shape=jax.ShapeDtypeStruct((t, IN_WIDTH), BF16),
        compiler_params=_params(("arbitrary",)),
        name="in_projection",
    )(x2d, g_norm, mod_l, w_in_bf16, cos_t, sin_t)


def _attn_kernel(q_ref, k_ref, v_ref, lq1_ref, lk1_ref, lq2_ref, lk2_ref, gs_ref, o_ref,
                 qm_ref, m_ref, l_ref, acc_ref, vt_ref, sa_ref, sb_ref, *, tq, n_chunks,
                 lam_init):
    qi = pl.program_id(2)

    @pl.when(qi == 0)
    def _():
        for j in range(n_chunks):
            vt_ref[j] = v_ref[j * tq:(j + 1) * tq, :].astype(F32).T.astype(BF16)

    q = q_ref[...]
    lane = lax.broadcasted_iota(jnp.int32, q.shape, 1)
    zero = jnp.zeros_like(q)
    qm_ref[0] = jnp.where(lane < HEAD_DIM, q, zero)
    qm_ref[1] = jnp.where(lane >= HEAD_DIM, q, zero)
    m_ref[...] = jnp.full(m_ref.shape, -jnp.inf, F32)
    l_ref[...] = jnp.zeros(l_ref.shape, F32)
    acc_ref[...] = jnp.zeros(acc_ref.shape, F32)

    def scores(j, s_ref):
        kc = k_ref[pl.ds(pl.multiple_of(j * tq, tq), tq), :]
        for c in range(2):
            s_ref[c] = lax.dot_general(kc, qm_ref[c], (((1,), (1,)), ((), ())),
                                       preferred_element_type=F32)

    def softmax_pv(j, s_ref, diagonal):
        vt = vt_ref[j]
        for c in range(2):
            s_t = s_ref[c]
            if diagonal:
                key = lax.broadcasted_iota(jnp.int32, s_t.shape, 0)
                qry = lax.broadcasted_iota(jnp.int32, s_t.shape, 1)
                s_t = jnp.where(key <= qry, s_t, -jnp.inf)
            m_old = m_ref[c]
            m_new = jnp.maximum(m_old, jnp.max(s_t, axis=0, keepdims=True))
            alpha = jnp.exp2(m_old - m_new)
            p = jnp.exp2(s_t - m_new)
            l_ref[c] = alpha * l_ref[c] + jnp.sum(p, axis=0, keepdims=True)
            acc_ref[c] = alpha * acc_ref[c] + jnp.dot(vt, p.astype(BF16),
                                                      preferred_element_type=F32)
            m_ref[c] = m_new

    scores(0, sa_ref)

    def pair(jj, carry):
        j = 2 * jj
        scores(j + 1, sb_ref)
        softmax_pv(j, sa_ref, False)
        scores(j + 2, sa_ref)
        softmax_pv(j + 1, sb_ref, False)
        return carry

    lax.fori_loop(0, lax.div(qi, 2), pair, 0)

    @pl.when(lax.rem(qi, 2) == 0)
    def _():
        softmax_pv(qi, sa_ref, True)

    @pl.when(lax.rem(qi, 2) == 1)
    def _():
        scores(qi, sb_ref)
        softmax_pv(qi - 1, sa_ref, False)
        softmax_pv(qi, sb_ref, True)

    lam = (jnp.exp(jnp.sum(lq1_ref[...] * lk1_ref[...], axis=-1, keepdims=True))
           - jnp.exp(jnp.sum(lq2_ref[...] * lk2_ref[...], axis=-1, keepdims=True)) + lam_init)
    o_t = acc_ref[0] * (1.0 / l_ref[0]) - lam * (acc_ref[1] * (1.0 / l_ref[1]))
    o = o_t.T
    o_ref[...] = (_rms_norm(o, gs_ref[...]) * (1.0 - lam_init)).astype(o_ref.dtype)


def _attention(proj, lam_vecs, g_subln, layer, bsz, seq):
    t = proj.shape[0]
    tq = min(TQ_ATTN, seq)
    nq = seq // tq
    lam_init = 0.8 - 0.6 * math.exp(-0.3 * layer)
    kern = functools.partial(_attn_kernel, tq=tq, n_chunks=nq, lam_init=lam_init)
    hb = D_MODEL // V_DIM
    vec64 = pl.BlockSpec((1, HEAD_DIM), lambda b, h, i: (0, 0))
    return pl.pallas_call(
        kern,
        grid=(bsz, N_HEADS, nq),
        in_specs=[
            pl.BlockSpec((tq, V_DIM), lambda b, h, i: (b * nq + i, COL_Q * hb + h)),
            pl.BlockSpec((seq, V_DIM), lambda b, h, i: (b, COL_K * hb + h)),
            pl.BlockSpec((seq, V_DIM), lambda b, h, i: (b, COL_V * hb + h)),
            vec64, vec64, vec64, vec64,
            pl.BlockSpec((1, V_DIM), lambda b, h, i: (0, 0)),
        ],
        out_specs=pl.BlockSpec((tq, V_DIM), lambda b, h, i: (b * nq + i, h)),
        out_shape=jax.ShapeDtypeStruct((t, D_MODEL), BF16),
        scratch_shapes=[
            pltpu.VMEM((2, tq, V_DIM), BF16),
            pltpu.VMEM((2, 1, tq), F32),
            pltpu.VMEM((2, 1, tq), F32),
            pltpu.VMEM((2, V_DIM, tq), F32),
            pltpu.VMEM((nq, V_DIM, tq), BF16),
            pltpu.VMEM((2, tq, tq), F32),
            pltpu.VMEM((2, tq, tq), F32),
        ],
        compiler_params=_params(("arbitrary", "arbitrary", "arbitrary")),
        name="diff_attention",
    )(proj, proj, proj, *lam_vecs, g_subln)


def _rnn_kernel(xr_ref, gy_ref, sa_ref, sr_ref, oat_ref, cw_ref, cb_ref, wg_ref, ba_ref, bi_ref,
                lam_ref, o_ref, xprev_ref, hprev_ref, *, tt):
    s = pl.program_id(1)

    @pl.when(s == 0)
    def _():
        xprev_ref[...] = jnp.zeros(xprev_ref.shape, F32)
        hprev_ref[...] = jnp.zeros(hprev_ref.shape, F32)

    neg = -lam_ref[...]
    softplus = jnp.maximum(neg, 0.0) + jnp.log1p(jnp.exp(-jnp.abs(neg)))
    groups = tt // CARRY_ROWS
    sub = lax.broadcasted_iota(jnp.int32, (groups, CARRY_ROWS, RNN_BLOCK), 1)

    for nb in range(RNN_BLOCKS):
        cols = slice(nb * RNN_BLOCK, (nb + 1) * RNN_BLOCK)
        x3 = xr_ref[:, cols].astype(F32).reshape(groups, CARRY_ROWS, RNN_BLOCK)
        prev = xprev_ref[:, cols]
        xprev_ref[:, cols] = x3[groups - 1]
        xc = cb_ref[:, cols] + cw_ref[RNN_CONV - 1:RNN_CONV, cols] * x3
        for k in range(1, RNN_CONV):
            tap = RNN_CONV - 1 - k
            xc = xc + cw_ref[tap:tap + 1, cols] * _shift_rows(x3, prev, k, sub)
        xb = xc.reshape(tt, RNN_BLOCK)
        g = jnp.dot(xb.astype(BF16), wg_ref[nb], preferred_element_type=F32)
        r = _sigmoid(g[:, :RNN_BLOCK] + ba_ref[:, cols])
        ig = _sigmoid(g[:, RNN_BLOCK:] + bi_ref[:, cols])
        log_a = (-RGLRU_C) * r * softplus[:, cols]
        a = jnp.exp(log_a)
        u = jnp.sqrt(-jnp.tanh(log_a) * (1.0 + a * a)) * (ig * xb)
        a = a.reshape(groups, CARRY_ROWS, RNN_BLOCK)
        u = u.reshape(groups, CARRY_ROWS, RNN_BLOCK)
        k = 1
        while k < CARRY_ROWS:
            live = sub >= k
            u = a * jnp.where(live, pltpu.roll(u, k, 1), 0.0) + u
            a = a * jnp.where(live, pltpu.roll(a, k, 1), 1.0)
            k *= 2
        carry = hprev_ref[:, cols]
        hs = []
        for gi in range(groups):
            hg = u[gi] + a[gi] * carry
            hs.append(hg)
            carry = hg[CARRY_ROWS - 1:CARRY_ROWS, :]
        hprev_ref[:, cols] = carry
        h = jnp.concatenate(hs, axis=0)
        o_rnn = h * gy_ref[:, cols].astype(F32)
        mixed = (sa_ref[:, cols].astype(F32) * oat_ref[:, cols].astype(F32)
                 + sr_ref[:, cols].astype(F32) * o_rnn)
        o_ref[:, cols] = mixed.astype(o_ref.dtype)


def _rnn_merge(proj, o_attn, conv_w, conv_b, w_gate_bf16, b_a, b_i, lam_rg, bsz, seq):
    t = proj.shape[0]
    tt = min(TT_RNN, seq)
    ns = seq // tt
    kern = functools.partial(_rnn_kernel, tt=tt)

    def col(cb):
        return pl.BlockSpec((tt, D_MODEL), lambda b, s: (b * ns + s, cb))

    row_vec = pl.BlockSpec((1, D_MODEL), lambda b, s: (0, 0))
    return pl.pallas_call(
        kern,
        grid=(bsz, ns),
        in_specs=[
            col(COL_XR), col(COL_YR), col(COL_GA), col(COL_GR),
            pl.BlockSpec((tt, D_MODEL), lambda b, s: (b * ns + s, 0)),
            pl.BlockSpec((RNN_CONV, D_MODEL), lambda b, s: (0, 0)),
            row_vec,
            pl.BlockSpec((RNN_BLOCKS, RNN_BLOCK, 2 * RNN_BLOCK), lambda b, s: (0, 0, 0)),
            row_vec, row_vec, row_vec,
        ],
        out_specs=pl.BlockSpec((tt, D_MODEL), lambda b, s: (b * ns + s, 0)),
        out_shape=jax.ShapeDtypeStruct((t, D_MODEL), BF16),
        scratch_shapes=[pltpu.VMEM((CARRY_ROWS, D_MODEL), F32),
                        pltpu.VMEM((1, D_MODEL), F32)],
        compiler_params=_params(("arbitrary", "arbitrary")),
        name="rglru_merge",
    )(proj, proj, proj, proj, o_attn, conv_w, conv_b, w_gate_bf16, b_a, b_i, lam_rg)


def _ffn_kernel(mix_ref, x_ref, mod_ref, wo_ref, g_ref, wu_ref, cw_ref, cb_ref, wd_ref, gf_ref,
                o_ref, carry_ref, *, tm, fc, tiles_per_batch, final):
    i = pl.program_id(0)
    b = lax.div(i, tiles_per_batch)
    first_tile = lax.rem(i, tiles_per_batch) == 0

    def mod(k):
        return mod_ref[pl.ds(b, 1), k * D_MODEL:(k + 1) * D_MODEL]

    attn = jnp.dot(mix_ref[...], wo_ref[...], preferred_element_type=F32)
    x1 = x_ref[...] + mod(2) * attn
    h = (_rms_norm(x1, g_ref[...]) * (1.0 + mod(4)) + mod(3)).astype(BF16)
    groups = tm // CARRY_ROWS
    sub = lax.broadcasted_iota(jnp.int32, (groups, CARRY_ROWS, fc), 1)

    def conv_branch(col0):
        cols = slice(col0, col0 + fc)
        up = jnp.dot(h, wu_ref[:, cols], preferred_element_type=F32)
        prev = jnp.where(first_tile, 0.0, carry_ref[:, cols])
        carry_ref[:, cols] = up[tm - CARRY_ROWS:, :]
        up3 = up.reshape(groups, CARRY_ROWS, fc)
        u = (cb_ref[:, cols] + cw_ref[2:3, cols] * up3
             + cw_ref[1:2, cols] * _shift_rows(up3, prev, 1, sub)
             + cw_ref[0:1, cols] * _shift_rows(up3, prev, 2, sub))
        return u.reshape(tm, fc)

    acc = None
    for j in range(D_FF // fc):
        val = conv_branch(j * fc)
        gate = conv_branch(D_FF + j * fc)
        hidden = (_gelu_tanh(gate) * val).astype(BF16)
        part = jnp.dot(hidden, wd_ref[j * fc:(j + 1) * fc, :], preferred_element_type=F32)
        acc = part if acc is None else acc + part

    x2 = x1 + mod(5) * acc
    if final:
        x2 = _rms_norm(x2, gf_ref[...])
    o_ref[...] = x2


def _out_ffn(mixed, x2d, mod_l, w_out_bf16, g_ffn, w_up_bf16, conv_w, conv_b, w_down_bf16,
             g_final, seq, final):
    t = x2d.shape[0]
    tm = min(TM_FFN, seq)
    kern = functools.partial(_ffn_kernel, tm=tm, fc=FC_FFN, tiles_per_batch=seq // tm, final=final)
    row_vec = pl.BlockSpec((1, D_MODEL), lambda i: (0, 0))
    return pl.pallas_call(
        kern,
        grid=(t // tm,),
        in_specs=[
            pl.BlockSpec((tm, D_MODEL), lambda i: (i, 0)),
            pl.BlockSpec((tm, D_MODEL), lambda i: (i, 0)),
            pl.BlockSpec((CARRY_ROWS, N_MOD * D_MODEL), lambda i: (0, 0)),
            _resident((D_MODEL, D_MODEL)),
            row_vec,
            _resident((D_MODEL, 2 * D_FF)),
            pl.BlockSpec((FFN_CONV, 2 * D_FF), lambda i: (0, 0)),
            pl.BlockSpec((1, 2 * D_FF), lambda i: (0, 0)),
            _resident((D_FF, D_MODEL)),
            row_vec,
        ],
        out_specs=pl.BlockSpec((tm, D_MODEL), lambda i: (i, 0)),
        out_shape=jax.ShapeDtypeStruct((t, D_MODEL), F32),
        scratch_shapes=[pltpu.VMEM((CARRY_ROWS, 2 * D_FF), F32)],
        compiler_params=_params(("arbitrary",)),
        name="outproj_ffn",
    )(mixed, x2d, mod_l, w_out_bf16, g_ffn, w_up_bf16, conv_w, conv_b, w_down_bf16, g_final)


def kernel(x, c, positions, w_mod, b_mod, g_norm_mix, w_in, lam_q1, lam_k1, lam_q2, lam_k2,
           g_subln, conv_rnn_w, conv_rnn_b, w_rg_a, b_rg_a, w_rg_i, b_rg_i, lam_rg,
           w_out, g_norm_ffn, w_up, conv_ffn_w, conv_ffn_b, w_down, g_final):
    bsz, seq, d = x.shape
    depth = w_mod.shape[0]
    assert d == D_MODEL and bsz <= CARRY_ROWS
    assert seq % max(TM_PROJ, TQ_ATTN, TT_RNN, TM_FFN) == 0
    t = bsz * seq

    c_pad = jnp.zeros((CARRY_ROWS, D_MODEL), F32).at[:bsz].set(c)
    mod = _modulation(c_pad, w_mod, b_mod)

    inv_freq = ROPE_THETA ** (-jnp.arange(0, HEAD_DIM, 2, dtype=F32) / HEAD_DIM)
    inv_freq_row = jnp.tile(inv_freq, V_DIM // (HEAD_DIM // 2)).reshape(1, V_DIM)
    cos_t, sin_t = _rope_tables(positions.reshape(t, 1), inv_freq_row)

    x2d = x.reshape(t, D_MODEL)
    for l in range(depth):
        proj = _in_projection(x2d, g_norm_mix[l].reshape(1, -1), mod[l], w_in[l].astype(BF16),
                              cos_t, sin_t, seq)
        lam_vecs = [v[l].reshape(1, HEAD_DIM) for v in (lam_q1, lam_k1, lam_q2, lam_k2)]
        o_attn = _attention(proj, lam_vecs, g_subln[l].reshape(1, V_DIM), l, bsz, seq)
        w_gate = jnp.concatenate([w_rg_a[l], w_rg_i[l]], axis=-1).astype(BF16)
        mixed = _rnn_merge(proj, o_attn, conv_rnn_w[l], conv_rnn_b[l].reshape(1, -1), w_gate,
                           b_rg_a[l].reshape(1, -1), b_rg_i[l].reshape(1, -1),
                           lam_rg[l].reshape(1, -1), bsz, seq)
        x2d = _out_ffn(mixed, x2d, mod[l], w_out[l].astype(BF16), g_norm_ffn[l].reshape(1, -1),
                       w_up[l].astype(BF16), conv_ffn_w[l], conv_ffn_b[l].reshape(1, -1),
                       w_down[l].astype(BF16), g_final.reshape(1, -1), seq, l == depth - 1)
    return x2d.reshape(bsz, seq, D_MODEL)
```

```python
import functools
import math

import jax
import jax.numpy as jnp
from jax import lax
from jax.experimental import pallas as pl
from jax.experimental.pallas import tpu as pltpu

F32 = jnp.float32
BF16 = jnp.bfloat16

D_MODEL = 1024
N_HEADS = 8
HEAD_DIM = 64
V_DIM = 128
RNN_BLOCKS = 8
RNN_BLOCK = 128
RNN_CONV = 4
RGLRU_C = 8.0
D_FF = 3 * D_MODEL
FFN_CONV = 3
ROPE_THETA = 10000.0
EPS = 1e-6
LOG2_E = math.log2(math.e)
N_MOD = 6
IN_WIDTH = 7 * D_MODEL
CARRY_ROWS = 8

COL_Q, COL_K, COL_V, COL_XR, COL_YR, COL_GA, COL_GR = range(7)

TM_PROJ = 512
NB_PROJ = 256
TQ_ATTN = 512
HG_ATTN = 4
TT_RNN = 256
TM_FFN = 512
FC_FFN = 1024
TM_ROPE = 2048
VMEM_LIMIT = 56 * 1024 * 1024


def _sigmoid(x):
    return 1.0 / (1.0 + jnp.exp(-x))


def _gelu_tanh(x):
    c = math.sqrt(2.0 / math.pi)
    return 0.5 * x * (1.0 + jnp.tanh(c * (x + 0.044715 * (x * x * x))))


def _rms_norm(x, g):
    ms = jnp.mean(x * x, axis=-1, keepdims=True)
    return x * lax.rsqrt(ms + EPS) * g


def _shift_rows(x3, prev, k, sub):
    r = pltpu.roll(x3, k, 1)
    before = jnp.concatenate([pltpu.roll(prev, k, 0)[None], r[:-1]], axis=0)
    return jnp.where(sub < k, before, r)


def _params(sem):
    return pltpu.CompilerParams(dimension_semantics=sem, vmem_limit_bytes=VMEM_LIMIT)


def _resident(shape):
    return pl.BlockSpec(shape, lambda *_: (0,) * len(shape), pipeline_mode=pl.Buffered(1))


def _mod_kernel(c_ref, w_ref, b_ref, o_ref):
    c = c_ref[...]
    o_ref[...] = jnp.dot(c * _sigmoid(c), w_ref[...], preferred_element_type=F32) + b_ref[...]


def _modulation(c_pad, w_mod, b_mod):
    depth = w_mod.shape[0]
    n_col = w_mod.shape[2] // D_MODEL
    return pl.pallas_call(
        _mod_kernel,
        grid=(depth, n_col),
        in_specs=[
            pl.BlockSpec((CARRY_ROWS, D_MODEL), lambda l, n: (0, 0)),
            pl.BlockSpec((None, D_MODEL, D_MODEL), lambda l, n: (l, 0, n)),
            pl.BlockSpec((None, 1, D_MODEL), lambda l, n: (l, 0, n)),
        ],
        out_specs=pl.BlockSpec((None, CARRY_ROWS, D_MODEL), lambda l, n: (l, 0, n)),
        out_shape=jax.ShapeDtypeStruct((depth, CARRY_ROWS, N_MOD * D_MODEL), F32),
        compiler_params=_params(("arbitrary", "arbitrary")),
        name="modulation",
    )(c_pad, w_mod, b_mod.reshape(depth, 1, -1))


def _rope_kernel(pos_ref, invf_ref, cos_ref, sin_ref):
    ang = pos_ref[...].astype(F32) * invf_ref[...]
    lane = lax.broadcasted_iota(jnp.int32, ang.shape, 1)
    s = jnp.sin(ang)
    cos_ref[...] = jnp.cos(ang)
    sin_ref[...] = jnp.where((lane & (HEAD_DIM // 2)) == 0, -s, s)


def _rope_tables(pos_col, inv_freq_row):
    t = pos_col.shape[0]
    tm = min(TM_ROPE, t)
    return pl.pallas_call(
        _rope_kernel,
        grid=(t // tm,),
        in_specs=[pl.BlockSpec((tm, 1), lambda i: (i, 0)),
                  pl.BlockSpec((1, V_DIM), lambda i: (0, 0))],
        out_specs=[pl.BlockSpec((tm, V_DIM), lambda i: (i, 0)),
                   pl.BlockSpec((tm, V_DIM), lambda i: (i, 0))],
        out_shape=[jax.ShapeDtypeStruct((t, V_DIM), F32)] * 2,
        compiler_params=_params(("arbitrary",)),
        name="rope_tables",
    )(pos_col, inv_freq_row)


def _inproj_kernel(x_ref, g_ref, mod_ref, w_ref, cos_ref, sin_ref, o_ref, *, tiles_per_batch, nb):
    b = lax.div(pl.program_id(0), tiles_per_batch)
    shift = mod_ref[pl.ds(b, 1), 0:D_MODEL]
    scale = mod_ref[pl.ds(b, 1), D_MODEL:2 * D_MODEL]
    h = (_rms_norm(x_ref[...], g_ref[...]) * (1.0 + scale) + shift).astype(BF16)

    cos = cos_ref[...]
    sin = sin_ref[...]
    lane = lax.broadcasted_iota(jnp.int32, cos.shape, 1)
    first_half = (lane & (HEAD_DIM // 2)) == 0
    q_scale = LOG2_E * HEAD_DIM ** -0.5

    for blk in range(IN_WIDTH // nb):
        cols = slice(blk * nb, (blk + 1) * nb)
        kind = (blk * nb) // D_MODEL
        acc = jnp.dot(h, w_ref[:, cols], preferred_element_type=F32)
        if kind in (COL_Q, COL_K):
            for g in range(nb // V_DIM):
                a = acc[:, g * V_DIM:(g + 1) * V_DIM]
                partner = jnp.where(first_half,
                                    pltpu.roll(a, V_DIM - HEAD_DIM // 2, 1),
                                    pltpu.roll(a, HEAD_DIM // 2, 1))
                r = a * cos + partner * sin
                if kind == COL_Q:
                    r = r * q_scale
                o_ref[:, blk * nb + g * V_DIM:blk * nb + (g + 1) * V_DIM] = r.astype(BF16)
        elif kind in (COL_V, COL_XR):
            o_ref[:, cols] = acc.astype(BF16)
        elif kind == COL_YR:
            o_ref[:, cols] = _gelu_tanh(acc).astype(BF16)
        else:
            o_ref[:, cols] = _sigmoid(acc).astype(BF16)


def _in_projection(x2d, g_norm, mod_l, w_in_bf16, cos_t, sin_t, seq):
    t = x2d.shape[0]
    tm = min(TM_PROJ, seq)
    kern = functools.partial(_inproj_kernel, tiles_per_batch=seq // tm, nb=NB_PROJ)
    return pl.pallas_call(
        kern,
        grid=(t // tm,),
        in_specs=[
            pl.BlockSpec((tm, D_MODEL), lambda i: (i, 0)),
            pl.BlockSpec((1, D_MODEL), lambda i: (0, 0)),
            pl.BlockSpec((CARRY_ROWS, N_MOD * D_MODEL), lambda i: (0, 0)),
            _resident((D_MODEL, IN_WIDTH)),
            pl.BlockSpec((tm, V_DIM), lambda i: (i, 0)),
            pl.BlockSpec((tm, V_DIM), lambda i: (i, 0)),
        ],
        out_specs=pl.BlockSpec((tm, IN_WIDTH), lambda i: (i, 0)),
        out_shape=jax.ShapeDtypeStruct((t, IN_WIDTH), BF16),
        compiler_params=_params(("arbitrary",)),
        name="in_projection",
    )(x2d, g_norm, mod_l, w_in_bf16, cos_t, sin_t)


def _attn_kernel(q_ref, k_ref, v_ref, lq1_ref, lk1_ref, lq2_ref, lk2_ref, gs_ref, o_ref,
                 qm_ref, m_ref, l_ref, acc_ref, vt_ref, sa_ref, sb_ref, ma_ref, mb_ref, *,
                 tq, n_chunks, lam_init):
    qi = pl.program_id(2)
    units = 2 * HG_ATTN

    @pl.when(qi == 0)
    def _():
        for hh in range(HG_ATTN):
            for j in range(n_chunks):
                vt_ref[hh, j] = v_ref[j * tq:(j + 1) * tq,
                                      hh * V_DIM:(hh + 1) * V_DIM].astype(F32).T.astype(BF16)

    for hh in range(HG_ATTN):
        q = q_ref[:, hh * V_DIM:(hh + 1) * V_DIM]
        lane = lax.broadcasted_iota(jnp.int32, q.shape, 1)
        zero = jnp.zeros_like(q)
        qm_ref[2 * hh] = jnp.where(lane < HEAD_DIM, q, zero)
        qm_ref[2 * hh + 1] = jnp.where(lane >= HEAD_DIM, q, zero)
    m_ref[...] = jnp.full(m_ref.shape, -jnp.inf, F32)
    l_ref[...] = jnp.zeros(l_ref.shape, F32)
    acc_ref[...] = jnp.zeros(acc_ref.shape, F32)

    def scores(j, s_ref, mx_ref):
        rows = pl.ds(pl.multiple_of(j * tq, tq), tq)
        for u in range(units):
            hh = u // 2
            kc = k_ref[rows, hh * V_DIM:(hh + 1) * V_DIM]
            s_t = lax.dot_general(kc, qm_ref[u], (((1,), (1,)), ((), ())),
                                  preferred_element_type=F32)
            s_ref[u] = s_t
            mx_ref[u] = jnp.max(s_t, axis=0, keepdims=True)

    def softmax_pv(j, s_ref, mx_ref, diagonal):
        for u in range(units):
            vt = vt_ref[u // 2, j]
            s_t = s_ref[u]
            m_old = m_ref[u]
            if diagonal:
                key = lax.broadcasted_iota(jnp.int32, s_t.shape, 0)
                qry = lax.broadcasted_iota(jnp.int32, s_t.shape, 1)
                s_t = jnp.where(key <= qry, s_t, -jnp.inf)
                m_new = jnp.maximum(m_old, jnp.max(s_t, axis=0, keepdims=True))
            else:
                m_new = jnp.maximum(m_old, mx_ref[u])
            alpha = jnp.exp2(m_old - m_new)
            p = jnp.exp2(s_t - m_new)
            l_ref[u] = alpha * l_ref[u] + jnp.sum(p, axis=0, keepdims=True)
            acc_ref[u] = alpha * acc_ref[u] + jnp.dot(vt, p.astype(BF16),
                                                      preferred_element_type=F32)
            m_ref[u] = m_new

    scores(0, sa_ref, ma_ref)

    def pair(jj, carry):
        j = 2 * jj
        scores(j + 1, sb_ref, mb_ref)
        softmax_pv(j, sa_ref, ma_ref, False)
        scores(j + 2, sa_ref, ma_ref)
        softmax_pv(j + 1, sb_ref, mb_ref, False)
        return carry

    lax.fori_loop(0, lax.div(qi, 2), pair, 0)

    @pl.when(lax.rem(qi, 2) == 0)
    def _():
        softmax_pv(qi, sa_ref, ma_ref, True)

    @pl.when(lax.rem(qi, 2) == 1)
    def _():
        scores(qi, sb_ref, mb_ref)
        softmax_pv(qi - 1, sa_ref, ma_ref, False)
        softmax_pv(qi, sb_ref, mb_ref, True)

    lam = (jnp.exp(jnp.sum(lq1_ref[...] * lk1_ref[...], axis=-1, keepdims=True))
           - jnp.exp(jnp.sum(lq2_ref[...] * lk2_ref[...], axis=-1, keepdims=True)) + lam_init)
    for hh in range(HG_ATTN):
        u0, u1 = 2 * hh, 2 * hh + 1
        o_t = acc_ref[u0] * (1.0 / l_ref[u0]) - lam * (acc_ref[u1] * (1.0 / l_ref[u1]))
        o = o_t.T
        o_ref[:, hh * V_DIM:(hh + 1) * V_DIM] = (
            _rms_norm(o, gs_ref[...]) * (1.0 - lam_init)).astype(o_ref.dtype)


def _attention(proj, lam_vecs, g_subln, layer, bsz, seq):
    t = proj.shape[0]
    tq = min(TQ_ATTN, seq)
    nq = seq // tq
    lam_init = 0.8 - 0.6 * math.exp(-0.3 * layer)
    kern = functools.partial(_attn_kernel, tq=tq, n_chunks=nq, lam_init=lam_init)
    gw = HG_ATTN * V_DIM
    gb = D_MODEL // gw
    units = 2 * HG_ATTN
    vec64 = pl.BlockSpec((1, HEAD_DIM), lambda b, h, i: (0, 0))

    def whole_seq(col_block):
        return pl.BlockSpec((seq, gw), lambda b, h, i: (b, col_block * gb + h),
                            pipeline_mode=pl.Buffered(1))

    return pl.pallas_call(
        kern,
        grid=(bsz, N_HEADS // HG_ATTN, nq),
        in_specs=[
            pl.BlockSpec((tq, gw), lambda b, h, i: (b * nq + i, COL_Q * gb + h)),
            whole_seq(COL_K),
            whole_seq(COL_V),
            vec64, vec64, vec64, vec64,
            pl.BlockSpec((1, V_DIM), lambda b, h, i: (0, 0)),
        ],
        out_specs=pl.BlockSpec((tq, gw), lambda b, h, i: (b * nq + i, h)),
        out_shape=jax.ShapeDtypeStruct((t, D_MODEL), BF16),
        scratch_shapes=[
            pltpu.VMEM((units, tq, V_DIM), BF16),
            pltpu.VMEM((units, 1, tq), F32),
            pltpu.VMEM((units, 1, tq), F32),
            pltpu.VMEM((units, V_DIM, tq), F32),
            pltpu.VMEM((HG_ATTN, nq, V_DIM, tq), BF16),
            pltpu.VMEM((units, tq, tq), F32),
            pltpu.VMEM((units, tq, tq), F32),
            pltpu.VMEM((units, 1, tq), F32),
            pltpu.VMEM((units, 1, tq), F32),
        ],
        compiler_params=_params(("arbitrary", "arbitrary", "arbitrary")),
        name="diff_attention",
    )(proj, proj, proj, *lam_vecs, g_subln)


def _rnn_kernel(xr_ref, gy_ref, sa_ref, sr_ref, oat_ref, cw_ref, cb_ref, wg_ref, ba_ref, bi_ref,
                lam_ref, o_ref, xprev_ref, hprev_ref, *, tt):
    s = pl.program_id(1)

    @pl.when(s == 0)
    def _():
        xprev_ref[...] = jnp.zeros(xprev_ref.shape, F32)
        hprev_ref[...] = jnp.zeros(hprev_ref.shape, F32)

    neg = -lam_ref[...]
    softplus = jnp.maximum(neg, 0.0) + jnp.log1p(jnp.exp(-jnp.abs(neg)))
    groups = tt // CARRY_ROWS
    sub = lax.broadcasted_iota(jnp.int32, (groups, CARRY_ROWS, RNN_BLOCK), 1)

    for nb in range(RNN_BLOCKS):
        cols = slice(nb * RNN_BLOCK, (nb + 1) * RNN_BLOCK)
        x3 = xr_ref[:, cols].astype(F32).reshape(groups, CARRY_ROWS, RNN_BLOCK)
        prev = xprev_ref[:, cols]
        xprev_ref[:, cols] = x3[groups - 1]
        xc = cb_ref[:, cols] + cw_ref[RNN_CONV - 1:RNN_CONV, cols] * x3
        for k in range(1, RNN_CONV):
            tap = RNN_CONV - 1 - k
            xc = xc + cw_ref[tap:tap + 1, cols] * _shift_rows(x3, prev, k, sub)
        xb = xc.reshape(tt, RNN_BLOCK)
        g = jnp.dot(xb.astype(BF16), wg_ref[nb], preferred_element_type=F32)
        r = _sigmoid(g[:, :RNN_BLOCK] + ba_ref[:, cols])
        ig = _sigmoid(g[:, RNN_BLOCK:] + bi_ref[:, cols])
        log_a = (-RGLRU_C) * r * softplus[:, cols]
        a = jnp.exp(log_a)
        u = jnp.sqrt(-jnp.tanh(log_a) * (1.0 + a * a)) * (ig * xb)
        a = a.reshape(groups, CARRY_ROWS, RNN_BLOCK)
        u = u.reshape(groups, CARRY_ROWS, RNN_BLOCK)
        k = 1
        while k < CARRY_ROWS:
            live = sub >= k
            u = a * jnp.where(live, pltpu.roll(u, k, 1), 0.0) + u
            a = a * jnp.where(live, pltpu.roll(a, k, 1), 1.0)
            k *= 2
        carry = hprev_ref[:, cols]
        hs = []
        for gi in range(groups):
            hg = u[gi] + a[gi] * carry
            hs.append(hg)
            carry = hg[CARRY_ROWS - 1:CARRY_ROWS, :]
        hprev_ref[:, cols] = carry
        h = jnp.concatenate(hs, axis=0)
        o_rnn = h * gy_ref[:, cols].astype(F32)
        mixed = (sa_ref[:, cols].astype(F32) * oat_ref[:, cols].astype(F32)
                 + sr_ref[:, cols].astype(F32) * o_rnn)
        o_ref[:, cols] = mixed.astype(o_ref.dtype)


def _rnn_merge(proj, o_attn, conv_w, conv_b, w_gate_bf16, b_a, b_i, lam_rg, bsz, seq):
    t = proj.shape[0]
    tt = min(TT_RNN, seq)
    ns = seq // tt
    kern = functools.partial(_rnn_kernel, tt=tt)

    def col(cb):
        return pl.BlockSpec((tt, D_MODEL), lambda b, s: (b * ns + s, cb))

    row_vec = pl.BlockSpec((1, D_MODEL), lambda b, s: (0, 0))
    return pl.pallas_call(
        kern,
        grid=(bsz, ns),
        in_specs=[
            col(COL_XR), col(COL_YR), col(COL_GA), col(COL_GR),
            pl.BlockSpec((tt, D_MODEL), lambda b, s: (b * ns + s, 0)),
            pl.BlockSpec((RNN_CONV, D_MODEL), lambda b, s: (0, 0)),
            row_vec,
            pl.BlockSpec((RNN_BLOCKS, RNN_BLOCK, 2 * RNN_BLOCK), lambda b, s: (0, 0, 0)),
            row_vec, row_vec, row_vec,
        ],
        out_specs=pl.BlockSpec((tt, D_MODEL), lambda b, s: (b * ns + s, 0)),
        out_shape=jax.ShapeDtypeStruct((t, D_MODEL), BF16),
        scratch_shapes=[pltpu.VMEM((CARRY_ROWS, D_MODEL), F32),
                        pltpu.VMEM((1, D_MODEL), F32)],
        compiler_params=_params(("arbitrary", "arbitrary")),
        name="rglru_merge",
    )(proj, proj, proj, proj, o_attn, conv_w, conv_b, w_gate_bf16, b_a, b_i, lam_rg)


def _ffn_kernel(mix_ref, x_ref, mod_ref, wo_ref, g_ref, wu_ref, cw_ref, cb_ref, wd_ref, gf_ref,
                o_ref, carry_ref, *, tm, fc, tiles_per_batch, final):
    i = pl.program_id(0)
    b = lax.div(i, tiles_per_batch)
    first_tile = lax.rem(i, tiles_per_batch) == 0

    def mod(k):
        return mod_ref[pl.ds(b, 1), k * D_MODEL:(k + 1) * D_MODEL]

    attn = jnp.dot(mix_ref[...], wo_ref[...], preferred_element_type=F32)
    x1 = x_ref[...] + mod(2) * attn
    h = (_rms_norm(x1, g_ref[...]) * (1.0 + mod(4)) + mod(3)).astype(BF16)
    groups = tm // CARRY_ROWS
    sub = lax.broadcasted_iota(jnp.int32, (groups, CARRY_ROWS, fc), 1)

    def conv_branch(col0):
        cols = slice(col0, col0 + fc)
        up = jnp.dot(h, wu_ref[:, cols], preferred_element_type=F32)
        prev = jnp.where(first_tile, 0.0, carry_ref[:, cols])
        carry_ref[:, cols] = up[tm - CARRY_ROWS:, :]
        up3 = up.reshape(groups, CARRY_ROWS, fc)
        u = (cb_ref[:, cols] + cw_ref[2:3, cols] * up3
             + cw_ref[1:2, cols] * _shift_rows(up3, prev, 1, sub)
             + cw_ref[0:1, cols] * _shift_rows(up3, prev, 2, sub))
        return u.reshape(tm, fc)

    acc = None
    for j in range(D_FF // fc):
        val = conv_branch(j * fc)
        gate = conv_branch(D_FF + j * fc)
        hidden = (_gelu_tanh(gate) * val).astype(BF16)
        part = jnp.dot(hidden, wd_ref[j * fc:(j + 1) * fc, :], preferred_element_type=F32)
        acc = part if acc is None else acc + part

    x2 = x1 + mod(5) * acc
    if final:
        x2 = _rms_norm(x2, gf_ref[...])
    o_ref[...] = x2


def _out_ffn(mixed, x2d, mod_l, w_out_bf16, g_ffn, w_up_bf16, conv_w, conv_b, w_down_bf16,
             g_final, seq, final):
    t = x2d.shape[0]
    tm = min(TM_FFN, seq)
    kern = functools.partial(_ffn_kernel, tm=tm, fc=FC_FFN, tiles_per_batch=seq // tm, final=final)
    row_vec = pl.BlockSpec((1, D_MODEL), lambda i: (0, 0))
    return pl.pallas_call(
        kern,
        grid=(t // tm,),
        in_specs=[
            pl.BlockSpec((tm, D_MODEL), lambda i: (i, 0)),
            pl.BlockSpec((tm, D_MODEL), lambda i: (i, 0)),
            pl.BlockSpec((CARRY_ROWS, N_MOD * D_MODEL), lambda i: (0, 0)),
            _resident((D_MODEL, D_MODEL)),
            row_vec,
            _resident((D_MODEL, 2 * D_FF)),
            pl.BlockSpec((FFN_CONV, 2 * D_FF), lambda i: (0, 0)),
            pl.BlockSpec((1, 2 * D_FF), lambda i: (0, 0)),
            _resident((D_FF, D_MODEL)),
            row_vec,
        ],
        out_specs=pl.BlockSpec((tm, D_MODEL), lambda i: (i, 0)),
        out_shape=jax.ShapeDtypeStruct((t, D_MODEL), F32),
        scratch_shapes=[pltpu.VMEM((CARRY_ROWS, 2 * D_FF), F32)],
        compiler_params=_params(("arbitrary",)),
        name="outproj_ffn",
    )(mixed, x2d, mod_l, w_out_bf16, g_ffn, w_up_bf16, conv_w, conv_b, w_down_bf16, g_final)


def kernel(x, c, positions, w_mod, b_mod, g_norm_mix, w_in, lam_q1, lam_k1, lam_q2, lam_k2,
           g_subln, conv_rnn_w, conv_rnn_b, w_rg_a, b_rg_a, w_rg_i, b_rg_i, lam_rg,
           w_out, g_norm_ffn, w_up, conv_ffn_w, conv_ffn_b, w_down, g_final):
    bsz, seq, d = x.shape
    depth = w_mod.shape[0]
    assert d == D_MODEL and bsz <= CARRY_ROWS
    assert seq % max(TM_PROJ, TQ_ATTN, TT_RNN, TM_FFN) == 0
    t = bsz * seq

    c_pad = jnp.zeros((CARRY_ROWS, D_MODEL), F32).at[:bsz].set(c)
    mod = _modulation(c_pad, w_mod, b_mod)

    inv_freq = ROPE_THETA ** (-jnp.arange(0, HEAD_DIM, 2, dtype=F32) / HEAD_DIM)
    inv_freq_row = jnp.tile(inv_freq, V_DIM // (HEAD_DIM // 2)).reshape(1, V_DIM)
    cos_t, sin_t = _rope_tables(positions.reshape(t, 1), inv_freq_row)

    x2d = x.reshape(t, D_MODEL)
    for l in range(depth):
        proj = _in_projection(x2d, g_norm_mix[l].reshape(1, -1), mod[l], w_in[l].astype(BF16),
                              cos_t, sin_t, seq)
        lam_vecs = [v[l].reshape(1, HEAD_DIM) for v in (lam_q1, lam_k1, lam_q2, lam_k2)]
        o_attn = _attention(proj, lam_vecs, g_subln[l].reshape(1, V_DIM), l, bsz, seq)
        w_gate = jnp.concatenate([w_rg_a[l], w_rg_i[l]], axis=-1).astype(BF16)
        mixed = _rnn_merge(proj, o_attn, conv_rnn_w[l], conv_rnn_b[l].reshape(1, -1), w_gate,
                           b_rg_a[l].reshape(1, -1), b_rg_i[l].reshape(1, -1),
                           lam_rg[l].reshape(1, -1), bsz, seq)
        x2d = _out_ffn(mixed, x2d, mod[l], w_out[l].astype(BF16), g_norm_ffn[l].reshape(1, -1),
                       w_up[l].astype(BF16), conv_ffn_w[l], conv_ffn_b[l].reshape(1, -1),
                       w_down[l].astype(BF16), g_final.reshape(1, -1), seq, l == depth - 1)
    return x2d.reshape(bsz, seq, D_MODEL)
```

```python
import functools
import math

import jax
import jax.numpy as jnp
from jax import lax
from jax.experimental import pallas as pl
from jax.experimental.pallas import tpu as pltpu

F32 = jnp.float32
BF16 = jnp.bfloat16

D_MODEL = 1024
N_HEADS = 8
HEAD_DIM = 64
V_DIM = 128
RNN_BLOCKS = 8
RNN_BLOCK = 128
RNN_CONV = 4
RGLRU_C = 8.0
D_FF = 3 * D_MODEL
FFN_CONV = 3
ROPE_THETA = 10000.0
EPS = 1e-6
LOG2_E = math.log2(math.e)
N_MOD = 6
IN_WIDTH = 7 * D_MODEL
CARRY_ROWS = 8

COL_Q, COL_K, COL_V, COL_XR, COL_YR, COL_GA, COL_GR = range(7)
OUT_Q, OUT_K, OUT_V, OUT_SA, OUT_RNN = range(5)
OUT_WIDTH = 5 * D_MODEL

TM_PROJ = 512
NB_PROJ = 256
TQ_ATTN = 512
HG_ATTN = 4
TM_FFN = 512
FC_FFN = 1024
SW_FFN = 256
TM_ROPE = 2048
VMEM_LIMIT = 56 * 1024 * 1024


def _sigmoid(x):
    return 1.0 / (1.0 + jnp.exp(-x))


def _gelu_tanh(x):
    c = math.sqrt(2.0 / math.pi)
    return 0.5 * x * (1.0 + jnp.tanh(c * (x + 0.044715 * (x * x * x))))


def _rms_norm(x, g):
    ms = jnp.mean(x * x, axis=-1, keepdims=True)
    return x * lax.rsqrt(ms + EPS) * g


def _shift_rows(x3, prev, k, sub):
    r = pltpu.roll(x3, k, 1)
    before = jnp.concatenate([pltpu.roll(prev, k, 0)[None], r[:-1]], axis=0)
    return jnp.where(sub < k, before, r)


def _params(sem):
    return pltpu.CompilerParams(dimension_semantics=sem, vmem_limit_bytes=VMEM_LIMIT)


def _resident(shape):
    return pl.BlockSpec(shape, lambda *_: (0,) * len(shape), pipeline_mode=pl.Buffered(1))


def _mod_kernel(c_ref, w_ref, b_ref, o_ref):
    c = c_ref[...]
    o_ref[...] = jnp.dot(c * _sigmoid(c), w_ref[...], preferred_element_type=F32) + b_ref[...]


def _modulation(c_pad, w_mod, b_mod):
    depth = w_mod.shape[0]
    n_col = w_mod.shape[2] // D_MODEL
    return pl.pallas_call(
        _mod_kernel,
        grid=(depth, n_col),
        in_specs=[
            pl.BlockSpec((CARRY_ROWS, D_MODEL), lambda l, n: (0, 0)),
            pl.BlockSpec((None, D_MODEL, D_MODEL), lambda l, n: (l, 0, n)),
            pl.BlockSpec((None, 1, D_MODEL), lambda l, n: (l, 0, n)),
        ],
        out_specs=pl.BlockSpec((None, CARRY_ROWS, D_MODEL), lambda l, n: (l, 0, n)),
        out_shape=jax.ShapeDtypeStruct((depth, CARRY_ROWS, N_MOD * D_MODEL), F32),
        compiler_params=_params(("arbitrary", "arbitrary")),
        name="modulation",
    )(c_pad, w_mod, b_mod.reshape(depth, 1, -1))


def _rope_kernel(pos_ref, invf_ref, cos_ref, sin_ref):
    ang = pos_ref[...].astype(F32) * invf_ref[...]
    lane = lax.broadcasted_iota(jnp.int32, ang.shape, 1)
    s = jnp.sin(ang)
    cos_ref[...] = jnp.cos(ang)
    sin_ref[...] = jnp.where((lane & (HEAD_DIM // 2)) == 0, -s, s)


def _rope_tables(pos_col, inv_freq_row):
    t = pos_col.shape[0]
    tm = min(TM_ROPE, t)
    return pl.pallas_call(
        _rope_kernel,
        grid=(t // tm,),
        in_specs=[pl.BlockSpec((tm, 1), lambda i: (i, 0)),
                  pl.BlockSpec((1, V_DIM), lambda i: (0, 0))],
        out_specs=[pl.BlockSpec((tm, V_DIM), lambda i: (i, 0)),
                   pl.BlockSpec((tm, V_DIM), lambda i: (i, 0))],
        out_shape=[jax.ShapeDtypeStruct((t, V_DIM), F32)] * 2,
        compiler_params=_params(("arbitrary",)),
        name="rope_tables",
    )(pos_col, inv_freq_row)


def _inproj_kernel(x_ref, g_ref, mod_ref, w_ref, cos_ref, sin_ref, cw_ref, cb_ref, wg_ref,
                   ba_ref, bi_ref, lam_ref, o_ref, xprev_ref, hprev_ref, *, tm, tiles_per_batch, nb):
    i = pl.program_id(0)
    b = lax.div(i, tiles_per_batch)

    @pl.when(lax.rem(i, tiles_per_batch) == 0)
    def _():
        xprev_ref[...] = jnp.zeros(xprev_ref.shape, F32)
        hprev_ref[...] = jnp.zeros(hprev_ref.shape, F32)

    shift = mod_ref[pl.ds(b, 1), 0:D_MODEL]
    scale = mod_ref[pl.ds(b, 1), D_MODEL:2 * D_MODEL]
    h = (_rms_norm(x_ref[...], g_ref[...]) * (1.0 + scale) + shift).astype(BF16)

    def proj(col0):
        return jnp.dot(h, w_ref[:, col0:col0 + nb], preferred_element_type=F32)

    neg = -lam_ref[...]
    softplus = jnp.maximum(neg, 0.0) + jnp.log1p(jnp.exp(-jnp.abs(neg)))
    groups = tm // CARRY_ROWS
    sub = lax.broadcasted_iota(jnp.int32, (groups, CARRY_ROWS, RNN_BLOCK), 1)

    def rnn_prep(blk):
        xr = proj(COL_XR * D_MODEL + blk * nb)
        gate_y = _gelu_tanh(proj(COL_YR * D_MODEL + blk * nb))
        gate_r = _sigmoid(proj(COL_GR * D_MODEL + blk * nb))
        return xr, gate_y, gate_r

    def rnn_part(blk, part, xr, gate_y, gate_r):
        n = blk * (nb // RNN_BLOCK) + part
        cols = slice(n * RNN_BLOCK, (n + 1) * RNN_BLOCK)
        pc = slice(part * RNN_BLOCK, (part + 1) * RNN_BLOCK)
        x3 = xr[:, pc].reshape(groups, CARRY_ROWS, RNN_BLOCK)
        prev = xprev_ref[:, cols]
        xprev_ref[:, cols] = x3[groups - 1]
        xc = cb_ref[:, cols] + cw_ref[RNN_CONV - 1:RNN_CONV, cols] * x3
        for k in range(1, RNN_CONV):
            tap = RNN_CONV - 1 - k
            xc = xc + cw_ref[tap:tap + 1, cols] * _shift_rows(x3, prev, k, sub)
        xb = xc.reshape(tm, RNN_BLOCK)
        g = jnp.dot(xb.astype(BF16), wg_ref[n], preferred_element_type=F32)
        r = _sigmoid(g[:, :RNN_BLOCK] + ba_ref[:, cols])
        ig = _sigmoid(g[:, RNN_BLOCK:] + bi_ref[:, cols])
        log_a = (-RGLRU_C) * r * softplus[:, cols]
        a = jnp.exp(log_a)
        u = jnp.sqrt(-jnp.tanh(log_a) * (1.0 + a * a)) * (ig * xb)
        a = a.reshape(groups, CARRY_ROWS, RNN_BLOCK)
        u = u.reshape(groups, CARRY_ROWS, RNN_BLOCK)
        k = 1
        while k < CARRY_ROWS:
            live = sub >= k
            u = a * jnp.where(live, pltpu.roll(u, k, 1), 0.0) + u
            a = a * jnp.where(live, pltpu.roll(a, k, 1), 1.0)
            k *= 2
        carry = hprev_ref[:, cols]
        hs = []
        for gi in range(groups):
            hg = u[gi] + a[gi] * carry
            hs.append(hg)
            carry = hg[CARRY_ROWS - 1:CARRY_ROWS, :]
        hprev_ref[:, cols] = carry
        hseq = jnp.concatenate(hs, axis=0)
        out = gate_r[:, pc] * (hseq * gate_y[:, pc])
        c0 = OUT_RNN * D_MODEL + n * RNN_BLOCK
        o_ref[:, c0:c0 + RNN_BLOCK] = out.astype(BF16)

    cos = cos_ref[...]
    sin = sin_ref[...]
    lane = lax.broadcasted_iota(jnp.int32, cos.shape, 1)
    first_half = (lane & (HEAD_DIM // 2)) == 0
    q_scale = LOG2_E * HEAD_DIM ** -0.5

    def rotary_store(acc, kind, out_blk, blk):
        for g in range(nb // V_DIM):
            a = acc[:, g * V_DIM:(g + 1) * V_DIM]
            partner = jnp.where(first_half,
                                pltpu.roll(a, V_DIM - HEAD_DIM // 2, 1),
                                pltpu.roll(a, HEAD_DIM // 2, 1))
            r = a * cos + partner * sin
            if kind == COL_Q:
                r = r * q_scale
            c0 = out_blk * D_MODEL + blk * nb + g * V_DIM
            o_ref[:, c0:c0 + V_DIM] = r.astype(BF16)

    def store(acc, out_blk, blk):
        c0 = out_blk * D_MODEL + blk * nb
        o_ref[:, c0:c0 + nb] = acc.astype(BF16)

    for blk in range(D_MODEL // nb):
        vals = rnn_prep(blk)
        rnn_part(blk, 0, *vals)
        rotary_store(proj(COL_Q * D_MODEL + blk * nb), COL_Q, OUT_Q, blk)
        rotary_store(proj(COL_K * D_MODEL + blk * nb), COL_K, OUT_K, blk)
        rnn_part(blk, 1, *vals)
        store(proj(COL_V * D_MODEL + blk * nb), OUT_V, blk)
        store(_sigmoid(proj(COL_GA * D_MODEL + blk * nb)), OUT_SA, blk)


def _in_projection(x2d, g_norm, mod_l, w_in_bf16, cos_t, sin_t, conv_w, conv_b, w_gate_bf16,
                   b_a, b_i, lam_rg, seq):
    t = x2d.shape[0]
    tm = min(TM_PROJ, seq)
    kern = functools.partial(_inproj_kernel, tm=tm, tiles_per_batch=seq // tm, nb=NB_PROJ)
    row_vec = pl.BlockSpec((1, D_MODEL), lambda i: (0, 0))
    return pl.pallas_call(
        kern,
        grid=(t // tm,),
        in_specs=[
            pl.BlockSpec((tm, D_MODEL), lambda i: (i, 0)),
            row_vec,
            pl.BlockSpec((CARRY_ROWS, N_MOD * D_MODEL), lambda i: (0, 0)),
            _resident((D_MODEL, IN_WIDTH)),
            pl.BlockSpec((tm, V_DIM), lambda i: (i, 0)),
            pl.BlockSpec((tm, V_DIM), lambda i: (i, 0)),
            pl.BlockSpec((RNN_CONV, D_MODEL), lambda i: (0, 0)),
            row_vec,
            pl.BlockSpec((RNN_BLOCKS, RNN_BLOCK, 2 * RNN_BLOCK), lambda i: (0, 0, 0)),
            row_vec, row_vec, row_vec,
        ],
        out_specs=pl.BlockSpec((tm, OUT_WIDTH), lambda i: (i, 0)),
        out_shape=jax.ShapeDtypeStruct((t, OUT_WIDTH), BF16),
        scratch_shapes=[pltpu.VMEM((CARRY_ROWS, D_MODEL), F32),
                        pltpu.VMEM((1, D_MODEL), F32)],
        compiler_params=_params(("arbitrary",)),
        name="in_projection",
    )(x2d, g_norm, mod_l, w_in_bf16, cos_t, sin_t, conv_w, conv_b, w_gate_bf16, b_a, b_i, lam_rg)


def _attn_kernel(q_ref, k_ref, v_ref, lq1_ref, lk1_ref, lq2_ref, lk2_ref, gs_ref, o_ref,
                 qm_ref, m_ref, l_ref, acc_ref, vt_ref, sa_ref, sb_ref, ma_ref, mb_ref, *,
                 tq, n_chunks, lam_init):
    qi = pl.program_id(2)
    units = 2 * HG_ATTN

    @pl.when(qi == 0)
    def _():
        for hh in range(HG_ATTN):
            for j in range(n_chunks):
                vt_ref[hh, j] = v_ref[j * tq:(j + 1) * tq,
                                      hh * V_DIM:(hh + 1) * V_DIM].astype(F32).T.astype(BF16)

    for hh in range(HG_ATTN):
        q = q_ref[:, hh * V_DIM:(hh + 1) * V_DIM]
        lane = lax.broadcasted_iota(jnp.int32, q.shape, 1)
        zero = jnp.zeros_like(q)
        qm_ref[2 * hh] = jnp.where(lane < HEAD_DIM, q, zero)
        qm_ref[2 * hh + 1] = jnp.where(lane >= HEAD_DIM, q, zero)
    m_ref[...] = jnp.full(m_ref.shape, -jnp.inf, F32)
    l_ref[...] = jnp.zeros(l_ref.shape, F32)
    acc_ref[...] = jnp.zeros(acc_ref.shape, F32)

    def scores(j, s_ref, mx_ref):
        rows = pl.ds(pl.multiple_of(j * tq, tq), tq)
        for u in range(units):
            hh = u // 2
            kc = k_ref[rows, hh * V_DIM:(hh + 1) * V_DIM]
            s_t = lax.dot_general(kc, qm_ref[u], (((1,), (1,)), ((), ())),
                                  preferred_element_type=F32)
            s_ref[u] = s_t
            mx_ref[u] = jnp.max(s_t, axis=0, keepdims=True)

    def softmax_pv(j, s_ref, mx_ref, diagonal):
        for u in range(units):
            vt = vt_ref[u // 2, j]
            s_t = s_ref[u]
            m_old = m_ref[u]
            if diagonal:
                key = lax.broadcasted_iota(jnp.int32, s_t.shape, 0)
                qry = lax.broadcasted_iota(jnp.int32, s_t.shape, 1)
                s_t = jnp.where(key <= qry, s_t, -jnp.inf)
                m_new = jnp.maximum(m_old, jnp.max(s_t, axis=0, keepdims=True))
            else:
                m_new = jnp.maximum(m_old, mx_ref[u])
            alpha = jnp.exp2(m_old - m_new)
            p = jnp.exp2(s_t - m_new)
            l_ref[u] = alpha * l_ref[u] + jnp.sum(p, axis=0, keepdims=True)
            acc_ref[u] = alpha * acc_ref[u] + jnp.dot(vt, p.astype(BF16),
                                                      preferred_element_type=F32)
            m_ref[u] = m_new

    scores(0, sa_ref, ma_ref)

    def pair(jj, carry):
        j = 2 * jj
        scores(j + 1, sb_ref, mb_ref)
        softmax_pv(j, sa_ref, ma_ref, False)
        scores(j + 2, sa_ref, ma_ref)
        softmax_pv(j + 1, sb_ref, mb_ref, False)
        return carry

    lax.fori_loop(0, lax.div(qi, 2), pair, 0)

    @pl.when(lax.rem(qi, 2) == 0)
    def _():
        softmax_pv(qi, sa_ref, ma_ref, True)

    @pl.when(lax.rem(qi, 2) == 1)
    def _():
        scores(qi, sb_ref, mb_ref)
        softmax_pv(qi - 1, sa_ref, ma_ref, False)
        softmax_pv(qi, sb_ref, mb_ref, True)

    lam = (jnp.exp(jnp.sum(lq1_ref[...] * lk1_ref[...], axis=-1, keepdims=True))
           - jnp.exp(jnp.sum(lq2_ref[...] * lk2_ref[...], axis=-1, keepdims=True)) + lam_init)
    for hh in range(HG_ATTN):
        u0, u1 = 2 * hh, 2 * hh + 1
        o_t = acc_ref[u0] * (1.0 / l_ref[u0]) - lam * (acc_ref[u1] * (1.0 / l_ref[u1]))
        o = o_t.T
        o_ref[:, hh * V_DIM:(hh + 1) * V_DIM] = (
            _rms_norm(o, gs_ref[...]) * (1.0 - lam_init)).astype(o_ref.dtype)


def _attention(proj, lam_vecs, g_subln, layer, bsz, seq):
    t = proj.shape[0]
    tq = min(TQ_ATTN, seq)
    nq = seq // tq
    lam_init = 0.8 - 0.6 * math.exp(-0.3 * layer)
    kern = functools.partial(_attn_kernel, tq=tq, n_chunks=nq, lam_init=lam_init)
    gw = HG_ATTN * V_DIM
    gb = D_MODEL // gw
    units = 2 * HG_ATTN
    vec64 = pl.BlockSpec((1, HEAD_DIM), lambda b, h, i: (0, 0))

    def whole_seq(col_block):
        return pl.BlockSpec((seq, gw), lambda b, h, i: (b, col_block * gb + h),
                            pipeline_mode=pl.Buffered(1))

    return pl.pallas_call(
        kern,
        grid=(bsz, N_HEADS // HG_ATTN, nq),
        in_specs=[
            pl.BlockSpec((tq, gw), lambda b, h, i: (b * nq + i, OUT_Q * gb + h)),
            whole_seq(OUT_K),
            whole_seq(OUT_V),
            vec64, vec64, vec64, vec64,
            pl.BlockSpec((1, V_DIM), lambda b, h, i: (0, 0)),
        ],
        out_specs=pl.BlockSpec((tq, gw), lambda b, h, i: (b * nq + i, h)),
        out_shape=jax.ShapeDtypeStruct((t, D_MODEL), BF16),
        scratch_shapes=[
            pltpu.VMEM((units, tq, V_DIM), BF16),
            pltpu.VMEM((units, 1, tq), F32),
            pltpu.VMEM((units, 1, tq), F32),
            pltpu.VMEM((units, V_DIM, tq), F32),
            pltpu.VMEM((HG_ATTN, nq, V_DIM, tq), BF16),
            pltpu.VMEM((units, tq, tq), F32),
            pltpu.VMEM((units, tq, tq), F32),
            pltpu.VMEM((units, 1, tq), F32),
            pltpu.VMEM((units, 1, tq), F32),
        ],
        compiler_params=_params(("arbitrary", "arbitrary", "arbitrary")),
        name="diff_attention",
    )(proj, proj, proj, *lam_vecs, g_subln)


def _ffn_kernel(sa_ref, rnn_ref, oat_ref, x_ref, mod_ref, wo_ref, g_ref, wu_ref, cw_ref, cb_ref,
                wd_ref, gf_ref, o_ref, carry_ref, hid_ref, *, tm, fc, sw, tiles_per_batch, final):
    i = pl.program_id(0)
    b = lax.div(i, tiles_per_batch)
    first_tile = lax.rem(i, tiles_per_batch) == 0

    def mod(k):
        return mod_ref[pl.ds(b, 1), k * D_MODEL:(k + 1) * D_MODEL]

    mixed = sa_ref[...].astype(F32) * oat_ref[...].astype(F32) + rnn_ref[...].astype(F32)
    attn = jnp.dot(mixed.astype(BF16), wo_ref[...], preferred_element_type=F32)
    x1 = x_ref[...] + mod(2) * attn
    h = (_rms_norm(x1, g_ref[...]) * (1.0 + mod(4)) + mod(3)).astype(BF16)
    groups = tm // CARRY_ROWS
    sub = lax.broadcasted_iota(jnp.int32, (groups, CARRY_ROWS, sw), 1)

    def up_proj(col0):
        return jnp.dot(h, wu_ref[:, col0:col0 + sw], preferred_element_type=F32)

    def conv(up, col0):
        cols = slice(col0, col0 + sw)
        prev = jnp.where(first_tile, 0.0, carry_ref[:, cols])
        carry_ref[:, cols] = up[tm - CARRY_ROWS:, :]
        up3 = up.reshape(groups, CARRY_ROWS, sw)
        u = (cb_ref[:, cols] + cw_ref[2:3, cols] * up3
             + cw_ref[1:2, cols] * _shift_rows(up3, prev, 1, sub)
             + cw_ref[0:1, cols] * _shift_rows(up3, prev, 2, sub))
        return u.reshape(tm, sw)

    n_slabs = D_FF // sw
    per_chunk = fc // sw
    acc = None
    ups = (up_proj(0), up_proj(D_FF))
    for s in range(n_slabs):
        nxt = None
        if s + 1 < n_slabs:
            nxt = (up_proj((s + 1) * sw), up_proj(D_FF + (s + 1) * sw))
        val = conv(ups[0], s * sw)
        gate = conv(ups[1], D_FF + s * sw)
        hid_ref[:, s * sw:(s + 1) * sw] = (_gelu_tanh(gate) * val).astype(BF16)
        if (s + 1) % per_chunk == 0:
            c0 = (s + 1 - per_chunk) * sw
            part = jnp.dot(hid_ref[:, c0:c0 + fc], wd_ref[c0:c0 + fc, :],
                           preferred_element_type=F32)
            acc = part if acc is None else acc + part
        ups = nxt

    x2 = x1 + mod(5) * acc
    if final:
        x2 = _rms_norm(x2, gf_ref[...])
    o_ref[...] = x2


def _out_ffn(proj, o_attn, x2d, mod_l, w_out_bf16, g_ffn, w_up_bf16, conv_w, conv_b, w_down_bf16,
             g_final, seq, final):
    t = x2d.shape[0]
    tm = min(TM_FFN, seq)
    kern = functools.partial(_ffn_kernel, tm=tm, fc=FC_FFN, sw=SW_FFN, tiles_per_batch=seq // tm,
                             final=final)
    row_vec = pl.BlockSpec((1, D_MODEL), lambda i: (0, 0))
    return pl.pallas_call(
        kern,
        grid=(t // tm,),
        in_specs=[
            pl.BlockSpec((tm, D_MODEL), lambda i: (i, OUT_SA)),
            pl.BlockSpec((tm, D_MODEL), lambda i: (i, OUT_RNN)),
            pl.BlockSpec((tm, D_MODEL), lambda i: (i, 0)),
            pl.BlockSpec((tm, D_MODEL), lambda i: (i, 0)),
            pl.BlockSpec((CARRY_ROWS, N_MOD * D_MODEL), lambda i: (0, 0)),
            _resident((D_MODEL, D_MODEL)),
            row_vec,
            _resident((D_MODEL, 2 * D_FF)),
            pl.BlockSpec((FFN_CONV, 2 * D_FF), lambda i: (0, 0)),
            pl.BlockSpec((1, 2 * D_FF), lambda i: (0, 0)),
            _resident((D_FF, D_MODEL)),
            row_vec,
        ],
        out_specs=pl.BlockSpec((tm, D_MODEL), lambda i: (i, 0)),
        out_shape=jax.ShapeDtypeStruct((t, D_MODEL), F32),
        scratch_shapes=[pltpu.VMEM((CARRY_ROWS, 2 * D_FF), F32),
                        pltpu.VMEM((tm, D_FF), BF16)],
        compiler_params=_params(("arbitrary",)),
        name="outproj_ffn",
    )(proj, proj, o_attn, x2d, mod_l, w_out_bf16, g_ffn, w_up_bf16, conv_w, conv_b, w_down_bf16,
      g_final)


def kernel(x, c, positions, w_mod, b_mod, g_norm_mix, w_in, lam_q1, lam_k1, lam_q2, lam_k2,
           g_subln, conv_rnn_w, conv_rnn_b, w_rg_a, b_rg_a, w_rg_i, b_rg_i, lam_rg,
           w_out, g_norm_ffn, w_up, conv_ffn_w, conv_ffn_b, w_down, g_final):
    bsz, seq, d = x.shape
    depth = w_mod.shape[0]
    assert d == D_MODEL and bsz <= CARRY_ROWS
    assert seq % max(TM_PROJ, TQ_ATTN, TM_FFN) == 0
    t = bsz * seq

    c_pad = jnp.zeros((CARRY_ROWS, D_MODEL), F32).at[:bsz].set(c)
    mod = _modulation(c_pad, w_mod, b_mod)

    inv_freq = ROPE_THETA ** (-jnp.arange(0, HEAD_DIM, 2, dtype=F32) / HEAD_DIM)
    inv_freq_row = jnp.tile(inv_freq, V_DIM // (HEAD_DIM // 2)).reshape(1, V_DIM)
    cos_t, sin_t = _rope_tables(positions.reshape(t, 1), inv_freq_row)

    x2d = x.reshape(t, D_MODEL)
    for l in range(depth):
        w_gate = jnp.concatenate([w_rg_a[l], w_rg_i[l]], axis=-1).astype(BF16)
        proj = _in_projection(x2d, g_norm_mix[l].reshape(1, -1), mod[l], w_in[l].astype(BF16),
                              cos_t, sin_t, conv_rnn_w[l], conv_rnn_b[l].reshape(1, -1), w_gate,
                              b_rg_a[l].reshape(1, -1), b_rg_i[l].reshape(1, -1),
                              lam_rg[l].reshape(1, -1), seq)
        lam_vecs = [v[l].reshape(1, HEAD_DIM) for v in (lam_q1, lam_k1, lam_q2, lam_k2)]
        o_attn = _attention(proj, lam_vecs, g_subln[l].reshape(1, V_DIM), l, bsz, seq)
        x2d = _out_ffn(proj, o_attn, x2d, mod[l], w_out[l].astype(BF16),
                       g_norm_ffn[l].reshape(1, -1), w_up[l].astype(BF16), conv_ffn_w[l],
                       conv_ffn_b[l].reshape(1, -1), w_down[l].astype(BF16),
                       g_final.reshape(1, -1), seq, l == depth - 1)
    return x2d.reshape(bsz, seq, D_MODEL)
```

```python
import functools
import math

import jax
import jax.numpy as jnp
from jax import lax
from jax.experimental import pallas as pl
from jax.experimental.pallas import tpu as pltpu

F32 = jnp.float32
BF16 = jnp.bfloat16

D_MODEL = 1024
N_HEADS = 8
HEAD_DIM = 64
V_DIM = 128
RNN_BLOCKS = 8
RNN_BLOCK = 128
RNN_CONV = 4
RGLRU_C = 8.0
D_FF = 3 * D_MODEL
FFN_CONV = 3
ROPE_THETA = 10000.0
EPS = 1e-6
LOG2_E = math.log2(math.e)
N_MOD = 6
IN_WIDTH = 7 * D_MODEL
CARRY_ROWS = 8

COL_Q, COL_K, COL_V, COL_XR, COL_YR, COL_GA, COL_GR = range(7)
OUT_Q, OUT_K, OUT_V, OUT_SA, OUT_RNN = range(5)
OUT_WIDTH = 5 * D_MODEL

TM_PROJ = 512
NB_PROJ = 256
TQ_ATTN = 512
QS_ATTN = 256
HG_ATTN = 4
TM_FFN = 512
FC_FFN = 1024
SW_FFN = 256
TM_ROPE = 2048
VMEM_LIMIT = 56 * 1024 * 1024


def _sigmoid(x):
    return 1.0 / (1.0 + jnp.exp(-x))


def _gelu_tanh(x):
    c = math.sqrt(2.0 / math.pi)
    return 0.5 * x * (1.0 + jnp.tanh(c * (x + 0.044715 * (x * x * x))))


def _rms_norm(x, g):
    ms = jnp.mean(x * x, axis=-1, keepdims=True)
    return x * lax.rsqrt(ms + EPS) * g


def _shift_rows(x3, prev, k, sub):
    r = pltpu.roll(x3, k, 1)
    before = jnp.concatenate([pltpu.roll(prev, k, 0)[None], r[:-1]], axis=0)
    return jnp.where(sub < k, before, r)


def _params(sem):
    return pltpu.CompilerParams(dimension_semantics=sem, vmem_limit_bytes=VMEM_LIMIT)


def _resident(shape):
    return pl.BlockSpec(shape, lambda *_: (0,) * len(shape), pipeline_mode=pl.Buffered(1))


def _mod_kernel(c_ref, w_ref, b_ref, o_ref):
    c = c_ref[...]
    o_ref[...] = jnp.dot(c * _sigmoid(c), w_ref[...], preferred_element_type=F32) + b_ref[...]


def _modulation(c_pad, w_mod, b_mod):
    depth = w_mod.shape[0]
    n_col = w_mod.shape[2] // D_MODEL
    return pl.pallas_call(
        _mod_kernel,
        grid=(depth, n_col),
        in_specs=[
            pl.BlockSpec((CARRY_ROWS, D_MODEL), lambda l, n: (0, 0)),
            pl.BlockSpec((None, D_MODEL, D_MODEL), lambda l, n: (l, 0, n)),
            pl.BlockSpec((None, 1, D_MODEL), lambda l, n: (l, 0, n)),
        ],
        out_specs=pl.BlockSpec((None, CARRY_ROWS, D_MODEL), lambda l, n: (l, 0, n)),
        out_shape=jax.ShapeDtypeStruct((depth, CARRY_ROWS, N_MOD * D_MODEL), F32),
        compiler_params=_params(("arbitrary", "arbitrary")),
        name="modulation",
    )(c_pad, w_mod, b_mod.reshape(depth, 1, -1))


def _rope_kernel(pos_ref, invf_ref, cos_ref, sin_ref):
    ang = pos_ref[...].astype(F32) * invf_ref[...]
    lane = lax.broadcasted_iota(jnp.int32, ang.shape, 1)
    s = jnp.sin(ang)
    cos_ref[...] = jnp.cos(ang)
    sin_ref[...] = jnp.where((lane & (HEAD_DIM // 2)) == 0, -s, s)


def _rope_tables(pos_col, inv_freq_row):
    t = pos_col.shape[0]
    tm = min(TM_ROPE, t)
    return pl.pallas_call(
        _rope_kernel,
        grid=(t // tm,),
        in_specs=[pl.BlockSpec((tm, 1), lambda i: (i, 0)),
                  pl.BlockSpec((1, V_DIM), lambda i: (0, 0))],
        out_specs=[pl.BlockSpec((tm, V_DIM), lambda i: (i, 0)),
                   pl.BlockSpec((tm, V_DIM), lambda i: (i, 0))],
        out_shape=[jax.ShapeDtypeStruct((t, V_DIM), F32)] * 2,
        compiler_params=_params(("arbitrary",)),
        name="rope_tables",
    )(pos_col, inv_freq_row)


def _inproj_kernel(x_ref, g_ref, mod_ref, w_ref, cos_ref, sin_ref, cw_ref, cb_ref, wg_ref,
                   ba_ref, bi_ref, lam_ref, o_ref, xprev_ref, hprev_ref, *, tm, tiles_per_batch, nb):
    i = pl.program_id(0)
    b = lax.div(i, tiles_per_batch)

    @pl.when(lax.rem(i, tiles_per_batch) == 0)
    def _():
        xprev_ref[...] = jnp.zeros(xprev_ref.shape, F32)
        hprev_ref[...] = jnp.zeros(hprev_ref.shape, F32)

    shift = mod_ref[pl.ds(b, 1), 0:D_MODEL]
    scale = mod_ref[pl.ds(b, 1), D_MODEL:2 * D_MODEL]
    h = (_rms_norm(x_ref[...], g_ref[...]) * (1.0 + scale) + shift).astype(BF16)

    def proj(col0):
        return jnp.dot(h, w_ref[:, col0:col0 + nb], preferred_element_type=F32)

    neg = -lam_ref[...]
    softplus = jnp.maximum(neg, 0.0) + jnp.log1p(jnp.exp(-jnp.abs(neg)))
    groups = tm // CARRY_ROWS
    sub = lax.broadcasted_iota(jnp.int32, (groups, CARRY_ROWS, RNN_BLOCK), 1)

    def rnn_prep(blk):
        xr = proj(COL_XR * D_MODEL + blk * nb)
        gate_y = _gelu_tanh(proj(COL_YR * D_MODEL + blk * nb))
        gate_r = _sigmoid(proj(COL_GR * D_MODEL + blk * nb))
        return xr, gate_y, gate_r

    def rnn_part(blk, part, xr, gate_y, gate_r):
        n = blk * (nb // RNN_BLOCK) + part
        cols = slice(n * RNN_BLOCK, (n + 1) * RNN_BLOCK)
        pc = slice(part * RNN_BLOCK, (part + 1) * RNN_BLOCK)
        x3 = xr[:, pc].reshape(groups, CARRY_ROWS, RNN_BLOCK)
        prev = xprev_ref[:, cols]
        xprev_ref[:, cols] = x3[groups - 1]
        xc = cb_ref[:, cols] + cw_ref[RNN_CONV - 1:RNN_CONV, cols] * x3
        for k in range(1, RNN_CONV):
            tap = RNN_CONV - 1 - k
            xc = xc + cw_ref[tap:tap + 1, cols] * _shift_rows(x3, prev, k, sub)
        xb = xc.reshape(tm, RNN_BLOCK)
        g = jnp.dot(xb.astype(BF16), wg_ref[n], preferred_element_type=F32)
        r = _sigmoid(g[:, :RNN_BLOCK] + ba_ref[:, cols])
        ig = _sigmoid(g[:, RNN_BLOCK:] + bi_ref[:, cols])
        log_a = (-RGLRU_C) * r * softplus[:, cols]
        a = jnp.exp(log_a)
        u = jnp.sqrt(-jnp.tanh(log_a) * (1.0 + a * a)) * (ig * xb)
        a = a.reshape(groups, CARRY_ROWS, RNN_BLOCK)
        u = u.reshape(groups, CARRY_ROWS, RNN_BLOCK)
        k = 1
        while k < CARRY_ROWS:
            live = sub >= k
            u = a * jnp.where(live, pltpu.roll(u, k, 1), 0.0) + u
            a = a * jnp.where(live, pltpu.roll(a, k, 1), 1.0)
            k *= 2
        carry = hprev_ref[:, cols]
        hs = []
        for gi in range(groups):
            hg = u[gi] + a[gi] * carry
            hs.append(hg)
            carry = hg[CARRY_ROWS - 1:CARRY_ROWS, :]
        hprev_ref[:, cols] = carry
        hseq = jnp.concatenate(hs, axis=0)
        out = gate_r[:, pc] * (hseq * gate_y[:, pc])
        c0 = OUT_RNN * D_MODEL + n * RNN_BLOCK
        o_ref[:, c0:c0 + RNN_BLOCK] = out.astype(BF16)

    cos = cos_ref[...]
    sin = sin_ref[...]
    lane = lax.broadcasted_iota(jnp.int32, cos.shape, 1)
    first_half = (lane & (HEAD_DIM // 2)) == 0
    q_scale = LOG2_E * HEAD_DIM ** -0.5

    def rotary_store(acc, kind, out_blk, blk):
        for g in range(nb // V_DIM):
            a = acc[:, g * V_DIM:(g + 1) * V_DIM]
            partner = jnp.where(first_half,
                                pltpu.roll(a, V_DIM - HEAD_DIM // 2, 1),
                                pltpu.roll(a, HEAD_DIM // 2, 1))
            r = a * cos + partner * sin
            if kind == COL_Q:
                r = r * q_scale
            c0 = out_blk * D_MODEL + blk * nb + g * V_DIM
            o_ref[:, c0:c0 + V_DIM] = r.astype(BF16)

    def store(acc, out_blk, blk):
        c0 = out_blk * D_MODEL + blk * nb
        o_ref[:, c0:c0 + nb] = acc.astype(BF16)

    for blk in range(D_MODEL // nb):
        vals = rnn_prep(blk)
        rnn_part(blk, 0, *vals)
        rotary_store(proj(COL_Q * D_MODEL + blk * nb), COL_Q, OUT_Q, blk)
        rotary_store(proj(COL_K * D_MODEL + blk * nb), COL_K, OUT_K, blk)
        rnn_part(blk, 1, *vals)
        store(proj(COL_V * D_MODEL + blk * nb), OUT_V, blk)
        store(_sigmoid(proj(COL_GA * D_MODEL + blk * nb)), OUT_SA, blk)


def _in_projection(x2d, g_norm, mod_l, w_in_bf16, cos_t, sin_t, conv_w, conv_b, w_gate_bf16,
                   b_a, b_i, lam_rg, seq):
    t = x2d.shape[0]
    tm = min(TM_PROJ, seq)
    kern = functools.partial(_inproj_kernel, tm=tm, tiles_per_batch=seq // tm, nb=NB_PROJ)
    row_vec = pl.BlockSpec((1, D_MODEL), lambda i: (0, 0))
    return pl.pallas_call(
        kern,
        grid=(t // tm,),
        in_specs=[
            pl.BlockSpec((tm, D_MODEL), lambda i: (i, 0)),
            row_vec,
            pl.BlockSpec((CARRY_ROWS, N_MOD * D_MODEL), lambda i: (0, 0)),
            _resident((D_MODEL, IN_WIDTH)),
            pl.BlockSpec((tm, V_DIM), lambda i: (i, 0)),
            pl.BlockSpec((tm, V_DIM), lambda i: (i, 0)),
            pl.BlockSpec((RNN_CONV, D_MODEL), lambda i: (0, 0)),
            row_vec,
            pl.BlockSpec((RNN_BLOCKS, RNN_BLOCK, 2 * RNN_BLOCK), lambda i: (0, 0, 0)),
            row_vec, row_vec, row_vec,
        ],
        out_specs=pl.BlockSpec((tm, OUT_WIDTH), lambda i: (i, 0)),
        out_shape=jax.ShapeDtypeStruct((t, OUT_WIDTH), BF16),
        scratch_shapes=[pltpu.VMEM((CARRY_ROWS, D_MODEL), F32),
                        pltpu.VMEM((1, D_MODEL), F32)],
        compiler_params=_params(("arbitrary",)),
        name="in_projection",
    )(x2d, g_norm, mod_l, w_in_bf16, cos_t, sin_t, conv_w, conv_b, w_gate_bf16, b_a, b_i, lam_rg)


def _attn_kernel(q_ref, qn_ref, k_ref, v_ref, lq1_ref, lk1_ref, lq2_ref, lk2_ref, gs_ref, o_ref,
                 qm_ref, m_ref, l_ref, acc_ref, vt_ref, sa_ref, sb_ref, ma_ref, mb_ref, *,
                 tq, n_chunks, lam_init):
    qi = pl.program_id(2)
    units = 2 * HG_ATTN
    strips = tq // QS_ATTN
    par = lax.rem(qi, 2)

    def mask_queries(src_ref, slot):
        for hh in range(HG_ATTN):
            q = src_ref[:, hh * V_DIM:(hh + 1) * V_DIM]
            lane = lax.broadcasted_iota(jnp.int32, q.shape, 1)
            zero = jnp.zeros_like(q)
            qm_ref[slot, 2 * hh] = jnp.where(lane < HEAD_DIM, q, zero)
            qm_ref[slot, 2 * hh + 1] = jnp.where(lane >= HEAD_DIM, q, zero)

    def scores_unit(u, st, j, s_ref, mx_ref, slot):
        rows = pl.ds(pl.multiple_of(j * tq, tq), tq)
        hh = u // 2
        qs = slice(st * QS_ATTN, (st + 1) * QS_ATTN)
        kc = k_ref[rows, hh * V_DIM:(hh + 1) * V_DIM]
        s_t = lax.dot_general(kc, qm_ref[slot, u, qs, :], (((1,), (1,)), ((), ())),
                              preferred_element_type=F32)
        s_ref[u, :, qs] = s_t
        mx_ref[u, :, qs] = jnp.max(s_t, axis=0, keepdims=True)

    def softmax_unit(u, st, j, s_ref, mx_ref, diagonal):
        qs = slice(st * QS_ATTN, (st + 1) * QS_ATTN)
        nk = (st + 1) * QS_ATTN if diagonal else tq
        vt = vt_ref[u // 2, j, :, 0:nk]
        s_t = s_ref[u, 0:nk, qs]
        m_old = m_ref[u, :, qs]
        if diagonal:
            key = lax.broadcasted_iota(jnp.int32, s_t.shape, 0)
            qry = lax.broadcasted_iota(jnp.int32, s_t.shape, 1) + st * QS_ATTN
            s_t = jnp.where(key <= qry, s_t, -jnp.inf)
            m_new = jnp.maximum(m_old, jnp.max(s_t, axis=0, keepdims=True))
        else:
            m_new = jnp.maximum(m_old, mx_ref[u, :, qs])
        alpha = jnp.exp2(m_old - m_new)
        p = jnp.exp2(s_t - m_new)
        l_ref[u, :, qs] = alpha * l_ref[u, :, qs] + jnp.sum(p, axis=0, keepdims=True)
        acc_ref[u, :, qs] = alpha * acc_ref[u, :, qs] + jnp.dot(vt, p.astype(BF16),
                                                            preferred_element_type=F32)
        m_ref[u, :, qs] = m_new

    @pl.when(qi == 0)
    def _():
        for hh in range(HG_ATTN):
            for j in range(n_chunks):
                vt_ref[hh, j] = v_ref[j * tq:(j + 1) * tq,
                                      hh * V_DIM:(hh + 1) * V_DIM].astype(F32).T.astype(BF16)
        mask_queries(q_ref, 0)
        for u in range(units):
            for st in range(strips):
                scores_unit(u, st, 0, sa_ref, ma_ref, 0)

    m_ref[...] = jnp.full(m_ref.shape, -jnp.inf, F32)
    l_ref[...] = jnp.zeros(l_ref.shape, F32)
    acc_ref[...] = jnp.zeros(acc_ref.shape, F32)

    def overlapped(j_scores, s_dst, m_dst, slot, j_soft, s_src, m_src, diagonal=False):
        for u in range(units):
            for st in range(strips):
                scores_unit(u, st, j_scores, s_dst, m_dst, slot)
                softmax_unit(u, st, j_soft, s_src, m_src, diagonal)

    def pair(jj, carry):
        j = 2 * jj
        overlapped(j + 1, sb_ref, mb_ref, par, j, sa_ref, ma_ref)
        overlapped(j + 2, sa_ref, ma_ref, par, j + 1, sb_ref, mb_ref)
        return carry

    lax.fori_loop(0, lax.div(qi, 2), pair, 0)

    @pl.when(par == 0)
    def _():
        mask_queries(qn_ref, 1 - par)
        for u in range(units):
            for st in range(strips):
                softmax_unit(u, st, qi, sa_ref, ma_ref, True)
                scores_unit(u, st, 0, sa_ref, ma_ref, 1 - par)

    @pl.when(par == 1)
    def _():
        mask_queries(qn_ref, 1 - par)
        overlapped(qi, sb_ref, mb_ref, par, qi - 1, sa_ref, ma_ref)
        overlapped(0, sa_ref, ma_ref, 1 - par, qi, sb_ref, mb_ref, diagonal=True)

    lam = (jnp.exp(jnp.sum(lq1_ref[...] * lk1_ref[...], axis=-1, keepdims=True))
           - jnp.exp(jnp.sum(lq2_ref[...] * lk2_ref[...], axis=-1, keepdims=True)) + lam_init)
    for hh in range(HG_ATTN):
        u0, u1 = 2 * hh, 2 * hh + 1
        o_t = acc_ref[u0] * (1.0 / l_ref[u0]) - lam * (acc_ref[u1] * (1.0 / l_ref[u1]))
        o = o_t.T
        o_ref[:, hh * V_DIM:(hh + 1) * V_DIM] = (
            _rms_norm(o, gs_ref[...]) * (1.0 - lam_init)).astype(o_ref.dtype)


def _attention(proj, lam_vecs, g_subln, layer, bsz, seq):
    t = proj.shape[0]
    tq = min(TQ_ATTN, seq)
    nq = seq // tq
    lam_init = 0.8 - 0.6 * math.exp(-0.3 * layer)
    kern = functools.partial(_attn_kernel, tq=tq, n_chunks=nq, lam_init=lam_init)
    gw = HG_ATTN * V_DIM
    gb = D_MODEL // gw
    units = 2 * HG_ATTN
    vec64 = pl.BlockSpec((1, HEAD_DIM), lambda b, h, i: (0, 0))

    def whole_seq(col_block):
        return pl.BlockSpec((seq, gw), lambda b, h, i: (b, col_block * gb + h),
                            pipeline_mode=pl.Buffered(1))

    return pl.pallas_call(
        kern,
        grid=(bsz, N_HEADS // HG_ATTN, nq),
        in_specs=[
            pl.BlockSpec((tq, gw), lambda b, h, i: (b * nq + i, OUT_Q * gb + h)),
            pl.BlockSpec((tq, gw),
                         lambda b, h, i: (b * nq + jnp.minimum(i + 1, nq - 1), OUT_Q * gb + h)),
            whole_seq(OUT_K),
            whole_seq(OUT_V),
            vec64, vec64, vec64, vec64,
            pl.BlockSpec((1, V_DIM), lambda b, h, i: (0, 0)),
        ],
        out_specs=pl.BlockSpec((tq, gw), lambda b, h, i: (b * nq + i, h)),
        out_shape=jax.ShapeDtypeStruct((t, D_MODEL), BF16),
        scratch_shapes=[
            pltpu.VMEM((2, units, tq, V_DIM), BF16),
            pltpu.VMEM((units, 1, tq), F32),
            pltpu.VMEM((units, 1, tq), F32),
            pltpu.VMEM((units, V_DIM, tq), F32),
            pltpu.VMEM((HG_ATTN, nq, V_DIM, tq), BF16),
            pltpu.VMEM((units, tq, tq), F32),
            pltpu.VMEM((units, tq, tq), F32),
            pltpu.VMEM((units, 1, tq), F32),
            pltpu.VMEM((units, 1, tq), F32),
        ],
        compiler_params=_params(("arbitrary", "arbitrary", "arbitrary")),
        name="diff_attention",
    )(proj, proj, proj, proj, *lam_vecs, g_subln)


def _ffn_kernel(sa_ref, rnn_ref, oat_ref, x_ref, mod_ref, wo_ref, g_ref, wu_ref, cw_ref, cb_ref,
                wd_ref, gf_ref, o_ref, carry_ref, hid_ref, *, tm, fc, sw, tiles_per_batch, final):
    i = pl.program_id(0)
    b = lax.div(i, tiles_per_batch)
    first_tile = lax.rem(i, tiles_per_batch) == 0

    def mod(k):
        return mod_ref[pl.ds(b, 1), k * D_MODEL:(k + 1) * D_MODEL]

    mixed = sa_ref[...].astype(F32) * oat_ref[...].astype(F32) + rnn_ref[...].astype(F32)
    attn = jnp.dot(mixed.astype(BF16), wo_ref[...], preferred_element_type=F32)
    x1 = x_ref[...] + mod(2) * attn
    h = (_rms_norm(x1, g_ref[...]) * (1.0 + mod(4)) + mod(3)).astype(BF16)
    groups = tm // CARRY_ROWS
    sub = lax.broadcasted_iota(jnp.int32, (groups, CARRY_ROWS, sw), 1)

    def up_proj(col0):
        return jnp.dot(h, wu_ref[:, col0:col0 + sw], preferred_element_type=F32)

    def conv(up, col0):
        cols = slice(col0, col0 + sw)
        prev = jnp.where(first_tile, 0.0, carry_ref[:, cols])
        carry_ref[:, cols] = up[tm - CARRY_ROWS:, :]
        up3 = up.reshape(groups, CARRY_ROWS, sw)
        u = (cb_ref[:, cols] + cw_ref[2:3, cols] * up3
             + cw_ref[1:2, cols] * _shift_rows(up3, prev, 1, sub)
             + cw_ref[0:1, cols] * _shift_rows(up3, prev, 2, sub))
        return u.reshape(tm, sw)

    n_slabs = D_FF // sw
    per_chunk = fc // sw
    acc = None
    ups = (up_proj(0), up_proj(D_FF))
    for s in range(n_slabs):
        nxt = None
        if s + 1 < n_slabs:
            nxt = (up_proj((s + 1) * sw), up_proj(D_FF + (s + 1) * sw))
        val = conv(ups[0], s * sw)
        gate = conv(ups[1], D_FF + s * sw)
        hid_ref[:, s * sw:(s + 1) * sw] = (_gelu_tanh(gate) * val).astype(BF16)
        if (s + 1) % per_chunk == 0:
            c0 = (s + 1 - per_chunk) * sw
            part = jnp.dot(hid_ref[:, c0:c0 + fc], wd_ref[c0:c0 + fc, :],
                           preferred_element_type=F32)
            acc = part if acc is None else acc + part
        ups = nxt

    x2 = x1 + mod(5) * acc
    if final:
        x2 = _rms_norm(x2, gf_ref[...])
    o_ref[...] = x2


def _out_ffn(proj, o_attn, x2d, mod_l, w_out_bf16, g_ffn, w_up_bf16, conv_w, conv_b, w_down_bf16,
             g_final, seq, final):
    t = x2d.shape[0]
    tm = min(TM_FFN, seq)
    kern = functools.partial(_ffn_kernel, tm=tm, fc=FC_FFN, sw=SW_FFN, tiles_per_batch=seq // tm,
                             final=final)
    row_vec = pl.BlockSpec((1, D_MODEL), lambda i: (0, 0))
    return pl.pallas_call(
        kern,
        grid=(t // tm,),
        in_specs=[
            pl.BlockSpec((tm, D_MODEL), lambda i: (i, OUT_SA)),
            pl.BlockSpec((tm, D_MODEL), lambda i: (i, OUT_RNN)),
            pl.BlockSpec((tm, D_MODEL), lambda i: (i, 0)),
            pl.BlockSpec((tm, D_MODEL), lambda i: (i, 0)),
            pl.BlockSpec((CARRY_ROWS, N_MOD * D_MODEL), lambda i: (0, 0)),
            _resident((D_MODEL, D_MODEL)),
            row_vec,
            _resident((D_MODEL, 2 * D_FF)),
            pl.BlockSpec((FFN_CONV, 2 * D_FF), lambda i: (0, 0)),
            pl.BlockSpec((1, 2 * D_FF), lambda i: (0, 0)),
            _resident((D_FF, D_MODEL)),
            row_vec,
        ],
        out_specs=pl.BlockSpec((tm, D_MODEL), lambda i: (i, 0)),
        out_shape=jax.ShapeDtypeStruct((t, D_MODEL), F32),
        scratch_shapes=[pltpu.VMEM((CARRY_ROWS, 2 * D_FF), F32),
                        pltpu.VMEM((tm, D_FF), BF16)],
        compiler_params=_params(("arbitrary",)),
        name="outproj_ffn",
    )(proj, proj, o_attn, x2d, mod_l, w_out_bf16, g_ffn, w_up_bf16, conv_w, conv_b, w_down_bf16,
      g_final)


def kernel(x, c, positions, w_mod, b_mod, g_norm_mix, w_in, lam_q1, lam_k1, lam_q2, lam_k2,
           g_subln, conv_rnn_w, conv_rnn_b, w_rg_a, b_rg_a, w_rg_i, b_rg_i, lam_rg,
           w_out, g_norm_ffn, w_up, conv_ffn_w, conv_ffn_b, w_down, g_final):
    bsz, seq, d = x.shape
    depth = w_mod.shape[0]
    assert d == D_MODEL and bsz <= CARRY_ROWS
    assert seq % max(TM_PROJ, TQ_ATTN, TM_FFN) == 0
    t = bsz * seq

    c_pad = jnp.zeros((CARRY_ROWS, D_MODEL), F32).at[:bsz].set(c)
    mod = _modulation(c_pad, w_mod, b_mod)

    inv_freq = ROPE_THETA ** (-jnp.arange(0, HEAD_DIM, 2, dtype=F32) / HEAD_DIM)
    inv_freq_row = jnp.tile(inv_freq, V_DIM // (HEAD_DIM // 2)).reshape(1, V_DIM)
    cos_t, sin_t = _rope_tables(positions.reshape(t, 1), inv_freq_row)

    x2d = x.reshape(t, D_MODEL)
    for l in range(depth):
        w_gate = jnp.concatenate([w_rg_a[l], w_rg_i[l]], axis=-1).astype(BF16)
        proj = _in_projection(x2d, g_norm_mix[l].reshape(1, -1), mod[l], w_in[l].astype(BF16),
                              cos_t, sin_t, conv_rnn_w[l], conv_rnn_b[l].reshape(1, -1), w_gate,
                              b_rg_a[l].reshape(1, -1), b_rg_i[l].reshape(1, -1),
                              lam_rg[l].reshape(1, -1), seq)
        lam_vecs = [v[l].reshape(1, HEAD_DIM) for v in (lam_q1, lam_k1, lam_q2, lam_k2)]
        o_attn = _attention(proj, lam_vecs, g_subln[l].reshape(1, V_DIM), l, bsz, seq)
        x2d = _out_ffn(proj, o_attn, x2d, mod[l], w_out[l].astype(BF16),
                       g_norm_ffn[l].reshape(1, -1), w_up[l].astype(BF16), conv_ffn_w[l],
                       conv_ffn_b[l].reshape(1, -1), w_down[l].astype(BF16),
                       g_final.reshape(1, -1), seq, l == depth - 1)
    return x2d.reshape(bsz, seq, D_MODEL)
```
